```python
import jax, jax.numpy as jnp
from jax import lax
import numpy as np

D_MODEL = 1024
BATCH = 8
SEQ = 4096
DEPTH = 1

MIX_WIDTH = D_MODEL
RWKV_WIDTH = MIX_WIDTH // 2
RWKV_HEAD_DIM = 64
RWKV_HEADS = RWKV_WIDTH // RWKV_HEAD_DIM
RWKV_DECAY_LORA = 64
RWKV_AAA_LORA = 64
RWKV_GATE_LORA = 128
RWKV_GN_EPS = 64e-5
GLA_WIDTH = MIX_WIDTH - RWKV_WIDTH
GLA_HEADS = 4
GLA_DV = GLA_WIDTH // GLA_HEADS
GLA_DK = GLA_DV // 2
GLA_KEY_WIDTH = GLA_HEADS * GLA_DK
GLA_GATE_LORA = 16
GLA_GATE_TEMP = 16.0
GLA_CHUNK = 64
CONV_WIDTH = 3
PLE_DIM = 256
D_FF = -(-8 * D_MODEL // (3 * 256)) * 256
NORM_EPS = 1e-6

RWKV_SPLITS = (RWKV_WIDTH, 2 * RWKV_WIDTH, 3 * RWKV_WIDTH,
               3 * RWKV_WIDTH + 2 * RWKV_DECAY_LORA,
               3 * RWKV_WIDTH + 2 * RWKV_DECAY_LORA + RWKV_AAA_LORA)
RWKV_IN_WIDTH = RWKV_SPLITS[-1] + RWKV_GATE_LORA
GLA_SPLITS = (2 * GLA_KEY_WIDTH + GLA_WIDTH, 2 * GLA_KEY_WIDTH + 2 * GLA_WIDTH)
GLA_IN_WIDTH = GLA_SPLITS[-1] + 2 * GLA_GATE_LORA
IN_WIDTH = RWKV_IN_WIDTH + GLA_IN_WIDTH

kernel_name = 'hymba_rwkv7_gla_bidir_encoder_layer'


def rms_norm(x, w, eps=NORM_EPS):
    xf = x.astype(jnp.float32)
    y = xf * lax.rsqrt(jnp.mean(xf * xf, axis=-1, keepdims=True) + eps)
    return (y * w.astype(jnp.float32)).astype(x.dtype)


def centred_shift(z):
    zp = jnp.pad(z, ((0, 0), (1, 1), (0, 0)))
    return 0.5 * (zp[:, :-2] + zp[:, 2:])


def centred_dwconv(z, w):
    pad = (w.shape[0] - 1) // 2
    return lax.conv_general_dilated(z, w.astype(z.dtype), window_strides=(1,),
                                    padding=[(pad, pad)],
                                    dimension_numbers=('NWC', 'WIO', 'NWC'),
                                    feature_group_count=z.shape[-1])


def rwkv7_scan(r, decay, k, v, kk, b, reverse):
    B, T, H, N = r.shape
    xs = tuple(jnp.moveaxis(t, 1, 0) for t in (r, decay, k, v, kk, b))

    def step(S, inp):
        r_t, w_t, k_t, v_t, kk_t, b_t = inp
        sa = jnp.einsum('bhij,bhj->bhi', S, -kk_t)
        S = (S * w_t[:, :, None, :] + sa[..., None] * b_t[:, :, None, :]
             + v_t[..., None] * k_t[:, :, None, :])
        return S, jnp.einsum('bhij,bhj->bhi', S, r_t)

    S0 = jnp.zeros((B, H, N, N), r.dtype)
    _, y = lax.scan(step, S0, xs, reverse=reverse)
    return jnp.moveaxis(y, 0, 1)


def rwkv7_mixer(z, w0, w_up, a0, a_up, g_up, k_k, k_a, r_k, ln_w, ln_b, out_dtype):
    B, T, _ = z.shape
    z = z.astype(jnp.float32)
    r, k, v, w_lo, a_lo, g_lo = jnp.split(z, RWKV_SPLITS, axis=-1)
    w_lo = w_lo.reshape(B, T, 2, RWKV_DECAY_LORA)
    w_log = -jax.nn.softplus(-(w0 + jnp.einsum('btdr,drc->btdc', jnp.tanh(w_lo), w_up))) - 0.5
    decay = jnp.exp(-jnp.exp(w_log))
    a = jax.nn.sigmoid(a0 + a_lo @ a_up)
    g = jax.nn.sigmoid(g_lo) @ g_up

    def heads(t):
        return t.reshape(B, T, RWKV_HEADS, RWKV_HEAD_DIM)

    kk = heads(k * k_k)
    kk = kk * lax.rsqrt(jnp.maximum(jnp.sum(kk * kk, axis=-1, keepdims=True), 1e-24))
    k = k * (1.0 + (a - 1.0) * k_a)
    r_h, k_h, v_h, a_h = heads(r), heads(k), heads(v), heads(a)
    b_h = kk * a_h
    y = (rwkv7_scan(r_h, heads(decay[:, :, 0]), k_h, v_h, kk, b_h, False)
         + rwkv7_scan(r_h, heads(decay[:, :, 1]), k_h, v_h, kk, b_h, True))
    mu = jnp.mean(y, axis=-1, keepdims=True)
    var = jnp.mean(jnp.square(y - mu), axis=-1, keepdims=True)
    y = heads(((y - mu) * lax.rsqrt(var + RWKV_GN_EPS)).reshape(B, T, RWKV_WIDTH) * ln_w + ln_b)
    bonus = jnp.sum(r_h * k_h * r_k, axis=-1, keepdims=True) * v_h
    y = (y + bonus).reshape(B, T, RWKV_WIDTH)
    return (y * g).astype(out_dtype)


def gla_chunked(q, k, v, log_a):
    B, H, T, DK = q.shape
    DV = v.shape[-1]
    nc = T // GLA_CHUNK

    def to_chunks(t):
        return t.reshape(B, H, nc, GLA_CHUNK, t.shape[-1]).transpose(2, 0, 1, 3, 4)

    qc, kc, vc = to_chunks(q), to_chunks(k), to_chunks(v)
    bc = jnp.cumsum(to_chunks(log_a), axis=-2)
    mask = jnp.tril(jnp.ones((GLA_CHUNK, GLA_CHUNK), dtype=bool))[:, :, None]

    def step(S, inp):
        q_c, k_c, v_c, b_c = inp
        o_inter = jnp.einsum('bhcd,bhde->bhce', q_c * jnp.exp(b_c), S)
        diff = b_c[:, :, :, None, :] - b_c[:, :, None, :, :]
        dmat = jnp.exp(jnp.where(mask, diff, -jnp.inf))
        scores = jnp.einsum('bhid,bhjd,bhijd->bhij', q_c, k_c, dmat)
        o_intra = jnp.einsum('bhij,bhje->bhie', scores, v_c)
        b_last = b_c[:, :, -1:, :]
        S = (S * jnp.exp(b_last[:, :, 0, :])[..., None]
             + jnp.einsum('bhcd,bhce->bhde', k_c * jnp.exp(b_last - b_c), v_c))
        return S, o_inter + o_intra

    S0 = jnp.zeros((B, H, DK, DV), q.dtype)
    _, o = lax.scan(step, S0, (qc, kc, vc, bc))
    return o.transpose(1, 2, 0, 3, 4).reshape(B, H, T, DV)


def gla_mixer(z, conv_w, a_up, a_b, norm_w, out_dtype):
    B, T, _ = z.shape
    z = z.astype(jnp.float32)
    qkv, g, a_lo = jnp.split(z, GLA_SPLITS, axis=-1)
    qkv = jax.nn.silu(centred_dwconv(qkv, conv_w))
    q, k, v = jnp.split(qkv, (GLA_KEY_WIDTH, 2 * GLA_KEY_WIDTH), axis=-1)
    a_lo = a_lo.reshape(B, T, 2, GLA_GATE_LORA)
    log_a = jax.nn.log_sigmoid(jnp.einsum('btdr,drc->btdc', a_lo, a_up) + a_b) / GLA_GATE_TEMP

    def heads(t, d):
        return t.reshape(B, T, GLA_HEADS, d).transpose(0, 2, 1, 3)

    def flip(t):
        return jnp.flip(t, axis=2)

    q_h = heads(q, GLA_DK) * GLA_DK ** -0.5
    k_h = heads(k, GLA_DK)
    v_h = heads(v, GLA_DV)
    o_f = gla_chunked(q_h, k_h, v_h, heads(log_a[:, :, 0], GLA_DK))
    o_b = flip(gla_chunked(flip(q_h), flip(k_h), flip(v_h), flip(heads(log_a[:, :, 1], GLA_DK))))
    o = (o_f + o_b).transpose(0, 2, 1, 3)
    o = rms_norm(o, norm_w) * jax.nn.silu(g.reshape(B, T, GLA_HEADS, GLA_DV))
    return o.reshape(B, T, GLA_WIDTH).astype(out_dtype)


def setup_inputs(seed: int = 0) -> dict:
    key = jax.random.key(seed)
    ks = jax.random.split(key, 32)
    L = DEPTH
    f32 = jnp.float32

    def nrm(k, shape, scale):
        return jax.random.normal(k, shape, f32) * scale

    def gain(k, shape):
        return 1.0 + 0.05 * jax.random.normal(k, shape, f32)

    return {
        'x': jax.random.normal(ks[0], (BATCH, SEQ, D_MODEL), f32),
        'p': jax.random.normal(ks[1], (DEPTH, BATCH, SEQ, PLE_DIM), f32),
        'norm_mix_pre': gain(ks[2], (L, D_MODEL)),
        'norm_mix_post': gain(ks[3], (L, D_MODEL)),
        'norm_ffn_pre': gain(ks[4], (L, D_MODEL)),
        'norm_ffn_post': gain(ks[5], (L, D_MODEL)),
        'norm_ple': gain(ks[6], (L, D_MODEL)),
        'w_in': nrm(ks[7], (L, D_MODEL, IN_WIDTH), D_MODEL ** -0.5),
        'rwkv_mu': jax.random.uniform(ks[8], (L, RWKV_IN_WIDTH), f32),
        'rwkv_w0': jax.random.uniform(ks[9], (L, 2, RWKV_WIDTH), f32, -6.0, -1.0),
        'rwkv_w_up': nrm(ks[10], (L, 2, RWKV_DECAY_LORA, RWKV_WIDTH), RWKV_DECAY_LORA ** -0.5),
        'rwkv_a0': nrm(ks[11], (L, RWKV_WIDTH), 0.3),
        'rwkv_a_up': nrm(ks[12], (L, RWKV_AAA_LORA, RWKV_WIDTH), 0.5 * RWKV_AAA_LORA ** -0.5),
        'rwkv_g_up': nrm(ks[13], (L, RWKV_GATE_LORA, RWKV_WIDTH), RWKV_GATE_LORA ** -0.5),
        'rwkv_k_k': 0.85 + 0.05 * jax.random.normal(ks[14], (L, RWKV_WIDTH), f32),
        'rwkv_k_a': gain(ks[15], (L, RWKV_WIDTH)),
        'rwkv_r_k': nrm(ks[16], (L, RWKV_HEADS, RWKV_HEAD_DIM), 0.1),
        'rwkv_ln_w': gain(ks[17], (L, RWKV_WIDTH)),
        'rwkv_ln_b': nrm(ks[18], (L, RWKV_WIDTH), 0.02),
        'gla_conv': nrm(ks[19], (L, CONV_WIDTH, 1, 2 * GLA_KEY_WIDTH + GLA_WIDTH), CONV_WIDTH ** -0.5),
        'gla_a_up': nrm(ks[20], (L, 2, GLA_GATE_LORA, GLA_KEY_WIDTH), GLA_GATE_LORA ** -0.5),
        'gla_a_b': 2.0 + 0.5 * jax.random.normal(ks[21], (L, 2, GLA_KEY_WIDTH), f32),
        'gla_norm': gain(ks[22], (L, GLA_DV)),
        'w_out': nrm(ks[23], (L, MIX_WIDTH, D_MODEL), MIX_WIDTH ** -0.5),
        'ffn_gate': nrm(ks[24], (L, D_MODEL, D_FF), D_MODEL ** -0.5),
        'ffn_up': nrm(ks[25], (L, D_MODEL, D_FF), D_MODEL ** -0.5),
        'ffn_down': nrm(ks[26], (L, D_FF, D_MODEL), D_FF ** -0.5),
        'ple_proj': nrm(ks[27], (L, PLE_DIM, D_MODEL), PLE_DIM ** -0.5),
        'ple_gate': nrm(ks[28], (L, D_MODEL, D_MODEL), D_MODEL ** -0.5),
        'ple_gate_b': nrm(ks[29], (L, D_MODEL), 0.02),
    }


def reference(x, p, norm_mix_pre, norm_mix_post, norm_ffn_pre, norm_ffn_post, norm_ple,
              w_in, rwkv_mu, rwkv_w0, rwkv_w_up, rwkv_a0, rwkv_a_up, rwkv_g_up,
              rwkv_k_k, rwkv_k_a, rwkv_r_k, rwkv_ln_w, rwkv_ln_b,
              gla_conv, gla_a_up, gla_a_b, gla_norm, w_out,
              ffn_gate, ffn_up, ffn_down, ple_proj, ple_gate, ple_gate_b):
    h = x
    for i in range(DEPTH):
        xn = rms_norm(h, norm_mix_pre[i])
        z = xn @ w_in[i]
        z_rwkv, z_gla = z[..., :RWKV_IN_WIDTH], z[..., RWKV_IN_WIDTH:]
        z_rwkv = z_rwkv + rwkv_mu[i] * (centred_shift(z_rwkv) - z_rwkv)
        y_rwkv = rwkv7_mixer(z_rwkv, rwkv_w0[i], rwkv_w_up[i], rwkv_a0[i], rwkv_a_up[i],
                             rwkv_g_up[i], rwkv_k_k[i], rwkv_k_a[i], rwkv_r_k[i],
                             rwkv_ln_w[i], rwkv_ln_b[i], h.dtype)
        y_gla = gla_mixer(z_gla, gla_conv[i], gla_a_up[i], gla_a_b[i], gla_norm[i], h.dtype)
        y = jnp.concatenate([y_rwkv, y_gla], axis=-1) @ w_out[i]
        h = h + rms_norm(y, norm_mix_post[i])
        hn = rms_norm(h, norm_ffn_pre[i])
        f = (jax.nn.silu(hn @ ffn_gate[i]) * (hn @ ffn_up[i])) @ ffn_down[i]
        h = h + rms_norm(f, norm_ffn_post[i])
        e = p[i] @ ple_proj[i]
        gate = jax.nn.sigmoid(h @ ple_gate[i] + ple_gate_b[i])
        h = h + rms_norm(gate * e, norm_ple[i])
    return h
```

```python
import functools

import jax
import jax.numpy as jnp
from jax import lax
from jax.experimental import pallas as pl
from jax.experimental.pallas import tpu as pltpu

F32 = jnp.float32
BF16 = jnp.bfloat16

NORM_EPS = 1e-6
RWKV_GN_EPS = 64e-5
RWKV_HEAD_DIM = 64
RWKV_DECAY_LORA = 64
RWKV_AAA_LORA = 64
RWKV_GATE_LORA = 128
GLA_HEADS = 4
GLA_GATE_LORA = 16
GLA_GATE_TEMP = 16.0
CHUNK = 64
LANES = 128
VMEM_LIMIT = 56 * 1024 * 1024


def _mm(a, b):
    return jnp.dot(a.astype(BF16), b.astype(BF16), preferred_element_type=F32)


def _mm_nt(a, b):
    return lax.dot_general(a.astype(BF16), b.astype(BF16), (((1,), (1,)), ((), ())),
                           preferred_element_type=F32)


def _mm_tn(a, b):
    return lax.dot_general(a.astype(BF16), b.astype(BF16), (((0,), (0,)), ((), ())),
                           preferred_element_type=F32)


def _split3(x):
    hi = x.astype(BF16)
    r1 = x - hi.astype(F32)
    mid = r1.astype(BF16)
    lo = (r1 - mid.astype(F32)).astype(BF16)
    return hi, mid, lo


def _mm_exact_lhs(a_bf16, x):
    hi, mid, lo = _split3(x)
    d = functools.partial(jnp.dot, preferred_element_type=F32)
    return d(a_bf16, hi) + d(a_bf16, mid) + d(a_bf16, lo)


def _mm_exact_rhs(x, b_bf16):
    hi, mid, lo = _split3(x)
    d = functools.partial(jnp.dot, preferred_element_type=F32)
    return d(hi, b_bf16) + d(mid, b_bf16) + d(lo, b_bf16)


def _rms(x, w):
    ms = jnp.mean(x * x, axis=-1, keepdims=True)
    return x * lax.rsqrt(ms + NORM_EPS) * w


def _softplus(u):
    return jnp.maximum(u, 0.0) + jnp.log(1.0 + jnp.exp(-jnp.abs(u)))


def _sigmoid(u):
    return 1.0 / (1.0 + jnp.exp(-u))


def _silu(u):
    return u * _sigmoid(u)


def _shifted(z, prev_row, next_row):
    tp = z.shape[0]
    row = lax.broadcasted_iota(jnp.int32, (tp, 1), 0)
    z_dn = jnp.where(row == 0, prev_row, pltpu.roll(z, 1, 0))
    z_up = jnp.where(row == tp - 1, next_row, pltpu.roll(z, tp - 1, 0))
    return z_dn, z_up


def _halo_rows(zp_ref, zn_ref):
    t = pl.program_id(1)
    nt = pl.num_programs(1)
    prev_row = jnp.where(t > 0, zp_ref[0, 7:8, :], 0.0)
    next_row = jnp.where(t < nt - 1, zn_ref[0, 0:1, :], 0.0)
    return prev_row, next_row


def _block_ones(width, seg):
    r = lax.broadcasted_iota(jnp.int32, (width, width), 0) // seg
    c = lax.broadcasted_iota(jnp.int32, (width, width), 1) // seg
    return (r == c).astype(BF16)


def _pair_stack(z, head0_lanes):
    zero = jnp.zeros_like(z)
    return jnp.concatenate([jnp.where(head0_lanes, z, zero), jnp.where(head0_lanes, zero, z)], axis=0)


def _pair_masks(reverse):
    n = 2 * CHUNK
    r = lax.broadcasted_iota(jnp.int32, (n, n), 0)
    c = lax.broadcasted_iota(jnp.int32, (n, n), 1)
    same = (r // CHUNK) == (c // CHUNK)
    tt = r % CHUNK
    ss = c % CHUNK
    if reverse:
        incl = same & (ss >= tt)
        strict = same & (ss > tt)
    else:
        incl = same & (ss <= tt)
        strict = same & (ss < tt)
    return incl, strict


def _chunk_tri(reverse):
    r = lax.broadcasted_iota(jnp.int32, (CHUNK, CHUNK), 0)
    c = lax.broadcasted_iota(jnp.int32, (CHUNK, CHUNK), 1)
    m = (c >= r) if reverse else (c <= r)
    return m.astype(BF16)


def _inproj_kernel(x_ref, nw_ref, wr_ref, wg_ref, zr_ref, zg_ref):
    xn = _rms(x_ref[...], nw_ref[...]).astype(BF16)
    zr_ref[...] = jnp.dot(xn, wr_ref[...], preferred_element_type=F32)
    zg_ref[...] = jnp.dot(xn, wg_ref[...], preferred_element_type=F32)


def _inproj(x2, nw, w_r, w_g, tm):
    m, d = x2.shape
    nr, ng = w_r.shape[1], w_g.shape[1]
    const = lambda i: (0, 0)
    return pl.pallas_call(
        _inproj_kernel,
        grid=(m // tm,),
        in_specs=[pl.BlockSpec((tm, d), lambda i: (i, 0)),
                  pl.BlockSpec((1, d), const),
                  pl.BlockSpec((d, nr), const),
                  pl.BlockSpec((d, ng), const)],
        out_specs=[pl.BlockSpec((tm, nr), lambda i: (i, 0)),
                   pl.BlockSpec((tm, ng), lambda i: (i, 0))],
        out_shape=[jax.ShapeDtypeStruct((m, nr), F32), jax.ShapeDtypeStruct((m, ng), F32)],
        compiler_params=pltpu.CompilerParams(dimension_semantics=("parallel",),
                                             vmem_limit_bytes=VMEM_LIMIT),
        name="inproj",
    )(x2, nw, w_r, w_g)


def _rwkv_prep_kernel(z_ref, zp_ref, zn_ref, mu_ref, w0_ref, wup_ref, a0_ref, aup_ref, gup_ref,
                      kk_ref, ka_ref, out_ref, *, width):
    w = width
    z = z_ref[0]
    prev_row, next_row = _halo_rows(zp_ref, zn_ref)
    z_dn, z_up = _shifted(z, prev_row, next_row)
    zs = z + mu_ref[...] * (0.5 * (z_dn + z_up) - z)
    r = zs[:, 0:w]
    k = zs[:, w:2 * w]
    v = zs[:, 2 * w:3 * w]
    w_lo = zs[:, 3 * w:3 * w + LANES]
    a_lo = zs[:, 3 * w + LANES:3 * w + 2 * LANES]
    g_lo = zs[:, 3 * w + 2 * LANES:3 * w + 3 * LANES]
    w_log = -_softplus(-(w0_ref[...] + _mm(jnp.tanh(w_lo), wup_ref[...]))) - 0.5
    lw = -jnp.exp(w_log)
    a = _sigmoid(a0_ref[...] + _mm(a_lo, aup_ref[...]))
    g = _mm(_sigmoid(g_lo), gup_ref[...])
    kk = k * kk_ref[...]
    ss = _mm_exact_rhs(kk * kk, _block_ones(w, RWKV_HEAD_DIM))
    kk = kk * lax.rsqrt(jnp.maximum(ss, 1e-24))
    k = k * (1.0 + (a - 1.0) * ka_ref[...])
    out_ref[0, :, 0:w] = r
    out_ref[0, :, w:2 * w] = k
    out_ref[0, :, 2 * w:3 * w] = v
    out_ref[0, :, 3 * w:4 * w] = kk
    out_ref[0, :, 4 * w:5 * w] = kk * a
    out_ref[0, :, 5 * w:7 * w] = lw
    out_ref[0, :, 7 * w:8 * w] = g


def _halo_specs(tp, width, t_len):
    nb8 = t_len // 8
    per = tp // 8
    main = pl.BlockSpec((1, tp, width), lambda b, t: (b, t, 0))
    prev = pl.BlockSpec((1, 8, width), lambda b, t: (b, jnp.maximum(t * per - 1, 0), 0))
    nxt = pl.BlockSpec((1, 8, width), lambda b, t: (b, jnp.minimum((t + 1) * per, nb8 - 1), 0))
    return [main, prev, nxt]


def _rwkv_prep(z_r, mu, w0, w_up, a0, a_up, g_up, k_k, k_a, tp, width):
    bsz, t_len, zw = z_r.shape
    const2 = lambda b, t: (0, 0)
    full = lambda arr: pl.BlockSpec(arr.shape, const2)
    return pl.pallas_call(
        functools.partial(_rwkv_prep_kernel, width=width),
        grid=(bsz, t_len // tp),
        in_specs=_halo_specs(tp, zw, t_len) + [full(mu), full(w0), full(w_up), full(a0), full(a_up),
                                               full(g_up), full(k_k), full(k_a)],
        out_specs=pl.BlockSpec((1, tp, 8 * width), lambda b, t: (b, t, 0)),
        out_shape=jax.ShapeDtypeStruct((bsz, t_len, 8 * width), F32),
        compiler_params=pltpu.CompilerParams(dimension_semantics=("parallel", "parallel"),
                                             vmem_limit_bytes=VMEM_LIMIT),
        name="rwkv_prep",
    )(z_r, z_r, z_r, mu, w0, w_up, a0, a_up, g_up, k_k, k_a)


def _neumann_inverse(a):
    n = a.shape[0]
    eye = (lax.broadcasted_iota(jnp.int32, (n, n), 0) == lax.broadcasted_iota(jnp.int32, (n, n), 1))
    t = a + eye.astype(F32)
    p = a
    steps = CHUNK.bit_length() - 2
    for _ in range(steps):
        p = _mm(p, p)
        t = t + _mm(t, p)
    return t


def _rwkv_scan_kernel(r_ref, k_ref, v_ref, kk_ref, b_ref, lw_ref, y_ref, ht_ref, *, reverse):
    @pl.when(pl.program_id(1) == 0)
    def _():
        ht_ref[...] = jnp.zeros_like(ht_ref)

    c = CHUNK
    n = 2 * c
    lw = lw_ref[0]
    g = _mm_exact_lhs(_chunk_tri(reverse), lw)
    total = jnp.sum(lw, axis=0, keepdims=True)
    eg = jnp.exp(g)
    einv = jnp.exp(-g)
    ec = jnp.exp(total - g)
    gam = jnp.exp(total)
    kk = kk_ref[0]
    bb = b_ref[0]
    kx = k_ref[0]
    r_t = r_ref[0] * eg
    a_t = -kk * jnp.exp(g - lw)
    b_t = bb * einv
    k_t = kx * einv
    b_h = bb * ec
    k_h = kx * ec
    v = v_ref[0]

    incl, strict = _pair_masks(reverse)
    head0 = lax.broadcasted_iota(jnp.int32, (c, LANES), 1) < RWKV_HEAD_DIM
    zero = jnp.zeros((n, n), F32)
    n_pairs = lw.shape[1] // LANES
    for p in range(n_pairs):
        sl = slice(LANES * p, LANES * (p + 1))
        st = lambda z: _pair_stack(z[:, sl], head0)
        a_p, r_p, b_p, k_p, v_p, bh_p, kh_p = (st(z) for z in (a_t, r_t, b_t, k_t, v, b_h, k_h))
        gmat = _mm_nt(jnp.concatenate([a_p, r_p], axis=0), jnp.concatenate([b_p, k_p], axis=0))
        a_ab = jnp.where(strict, gmat[:n, :n], zero)
        a_ak = jnp.where(strict, gmat[:n, n:], zero)
        a_rb = jnp.where(incl, gmat[n:, :n], zero)
        a_rk = jnp.where(incl, gmat[n:, n:], zero)
        tinv = _neumann_inverse(a_ab)
        w12 = _mm(tinv, jnp.concatenate([a_p, _mm(a_ak, v_p)], axis=1))
        rw = _mm(a_rb, w12)
        r_hat = r_p + rw[:, :n]
        y0 = rw[:, n:] + _mm(a_rk, v_p)
        m12 = _mm_tn(w12, bh_p)
        m1 = m12[:n]
        m2 = m12[n:] + _mm_tn(v_p, kh_p)
        ht = ht_ref[p]
        y_p = _mm_nt(r_hat, ht) + y0
        y_ref[0, :, sl] = y_p[:c] + y_p[c:]
        ht_ref[p] = ht * gam[:, sl] + _mm(ht, m1) + m2


def _rwkv_scan(rw, width, reverse):
    bsz, t_len, _ = rw.shape
    nc = t_len // CHUNK
    tmap = (lambda t: nc - 1 - t) if reverse else (lambda t: t)
    col = lambda j: pl.BlockSpec((1, CHUNK, width), lambda b, t: (b, tmap(t), j))
    lw_col = 6 if reverse else 5
    n_pairs = width // LANES
    return pl.pallas_call(
        functools.partial(_rwkv_scan_kernel, reverse=reverse),
        grid=(bsz, nc),
        in_specs=[col(0), col(1), col(2), col(3), col(4), col(lw_col)],
        out_specs=pl.BlockSpec((1, CHUNK, width), lambda b, t: (b, tmap(t), 0)),
        out_shape=jax.ShapeDtypeStruct((bsz, t_len, width), F32),
        scratch_shapes=[pltpu.VMEM((n_pairs, 2 * CHUNK, 2 * CHUNK), F32)],
        compiler_params=pltpu.CompilerParams(dimension_semantics=("parallel", "arbitrary"),
                                             vmem_limit_bytes=VMEM_LIMIT),
        name="rwkv_scan_bwd" if reverse else "rwkv_scan_fwd",
    )(rw, rw, rw, rw, rw, rw)


def _rwkv_post_kernel(yf_ref, yb_ref, r_ref, k_ref, v_ref, g_ref, rk_ref, lnw_ref, lnb_ref, out_ref):
    w = yf_ref.shape[-1]
    ones = _block_ones(w, RWKV_HEAD_DIM)
    inv_n = 1.0 / RWKV_HEAD_DIM
    y = yf_ref[0] + yb_ref[0]
    mu = _mm_exact_rhs(y, ones) * inv_n
    yc = y - mu
    var = _mm_exact_rhs(yc * yc, ones) * inv_n
    yn = yc * lax.rsqrt(var + RWKV_GN_EPS) * lnw_ref[...] + lnb_ref[...]
    bonus = _mm_exact_rhs(r_ref[0] * k_ref[0] * rk_ref[...], ones) * v_ref[0]
    out_ref[0] = ((yn + bonus) * g_ref[0]).astype(out_ref.dtype)


def _rwkv_post(y_f, y_b, rw, r_k, ln_w, ln_b, tp):
    bsz, t_len, width = y_f.shape
    blk = lambda j: pl.BlockSpec((1, tp, width), lambda b, t: (b, t, j))
    vec = pl.BlockSpec((1, width), lambda b, t: (0, 0))
    return pl.pallas_call(
        _rwkv_post_kernel,
        grid=(bsz, t_len // tp),
        in_specs=[blk(0), blk(0), blk(0), blk(1), blk(2), blk(7), vec, vec, vec],
        out_specs=blk(0),
        out_shape=jax.ShapeDtypeStruct((bsz, t_len, width), BF16),
        compiler_params=pltpu.CompilerParams(dimension_semantics=("parallel", "parallel"),
                                             vmem_limit_bytes=VMEM_LIMIT),
        name="rwkv_post",
    )(y_f, y_b, rw, rw, rw, rw, r_k, ln_w, ln_b)


def _gla_prep_kernel(z_ref, zp_ref, zn_ref, cw_ref, aup_ref, ab_ref, out_ref, *, kw, vw):
    qkv_w = 2 * kw + vw
    z = z_ref[0]
    prev_row, next_row = _halo_rows(zp_ref, zn_ref)
    x = z[:, :qkv_w]
    x_dn, x_up = _shifted(x, prev_row[:, :qkv_w], next_row[:, :qkv_w])
    qkv = _silu(cw_ref[0:1, :] * x_dn + cw_ref[1:2, :] * x + cw_ref[2:3, :] * x_up)
    a_lo = z[:, qkv_w + vw:qkv_w + vw + LANES]
    logit = _mm(a_lo, aup_ref[...]) + ab_ref[...]
    log_a = -_softplus(-logit) * (1.0 / GLA_GATE_TEMP)
    dk = kw // GLA_HEADS
    out_ref[0, :, 0:kw] = qkv[:, 0:kw] * (dk ** -0.5)
    out_ref[0, :, kw:qkv_w] = qkv[:, kw:qkv_w]
    out_ref[0, :, qkv_w:qkv_w + 2 * kw] = log_a


def _gla_prep(z_g, conv_w, a_up, a_b, tp, kw, vw):
    bsz, t_len, zw = z_g.shape
    ow = 4 * kw + vw
    const2 = lambda b, t: (0, 0)
    full = lambda arr: pl.BlockSpec(arr.shape, const2)
    return pl.pallas_call(
        functools.partial(_gla_prep_kernel, kw=kw, vw=vw),
        grid=(bsz, t_len // tp),
        in_specs=_halo_specs(tp, zw, t_len) + [full(conv_w), full(a_up), full(a_b)],
        out_specs=pl.BlockSpec((1, tp, ow), lambda b, t: (b, t, 0)),
        out_shape=jax.ShapeDtypeStruct((bsz, t_len, ow), F32),
        compiler_params=pltpu.CompilerParams(dimension_semantics=("parallel", "parallel"),
                                             vmem_limit_bytes=VMEM_LIMIT),
        name="gla_prep",
    )(z_g, z_g, z_g, conv_w, a_up, a_b)


def _gla_scan_kernel(q_ref, k_ref, v_ref, la_ref, o_ref, st_ref, *, reverse):
    @pl.when(pl.program_id(1) == 0)
    def _():
        st_ref[...] = jnp.zeros_like(st_ref)

    c = CHUNK
    la = la_ref[0]
    bc = _mm_exact_lhs(_chunk_tri(reverse), la)
    total = jnp.sum(la, axis=0, keepdims=True)
    q_t = q_ref[0] * jnp.exp(bc)
    k_t = k_ref[0] * jnp.exp(-bc)
    k_h = k_ref[0] * jnp.exp(total - bc)
    gam = jnp.exp(total)
    v = v_ref[0]

    r = lax.broadcasted_iota(jnp.int32, (c, c), 0)
    cc = lax.broadcasted_iota(jnp.int32, (c, c), 1)
    causal = (cc >= r) if reverse else (cc <= r)
    head0 = lax.broadcasted_iota(jnp.int32, (c, LANES), 1) < (LANES // 2)
    zero = jnp.zeros((c, LANES), F32)
    n_pairs = la.shape[1] // LANES
    for p in range(n_pairs):
        sl = slice(LANES * p, LANES * (p + 1))
        st = st_ref[p]
        st_new = st * gam[:, sl]
        for h in range(2):
            hm = head0 if h == 0 else jnp.logical_not(head0)
            vsl = slice(LANES * (2 * p + h), LANES * (2 * p + h + 1))
            v_h = v[:, vsl]
            q_h = jnp.where(hm, q_t[:, sl], zero)
            scores = jnp.where(causal, _mm_nt(q_h, k_t[:, sl]), 0.0)
            o_ref[0, :, vsl] = _mm(scores, v_h) + _mm_nt(q_h, st)
            st_new = st_new + _mm_tn(v_h, jnp.where(hm, k_h[:, sl], zero))
        st_ref[p] = st_new


def _gla_scan(gl, kw, vw, reverse):
    bsz, t_len, _ = gl.shape
    nc = t_len // CHUNK
    tmap = (lambda t: nc - 1 - t) if reverse else (lambda t: t)
    blk = lambda w, j: pl.BlockSpec((1, CHUNK, w), lambda b, t: (b, tmap(t), j))
    la_col = (2 * kw + vw) // kw + (1 if reverse else 0)
    return pl.pallas_call(
        functools.partial(_gla_scan_kernel, reverse=reverse),
        grid=(bsz, nc),
        in_specs=[blk(kw, 0), blk(kw, 1), blk(vw, (2 * kw) // vw), blk(kw, la_col)],
        out_specs=pl.BlockSpec((1, CHUNK, vw), lambda b, t: (b, tmap(t), 0)),
        out_shape=jax.ShapeDtypeStruct((bsz, t_len, vw), F32),
        scratch_shapes=[pltpu.VMEM((kw // LANES, LANES, LANES), F32)],
        compiler_params=pltpu.CompilerParams(dimension_semantics=("parallel", "arbitrary"),
                                             vmem_limit_bytes=VMEM_LIMIT),
        name="gla_scan_bwd" if reverse else "gla_scan_fwd",
    )(gl, gl, gl, gl)


def _gla_post_kernel(of_ref, ob_ref, g_ref, nw_ref, out_ref):
    o = of_ref[0] + ob_ref[0]
    g = g_ref[0]
    for h in range(GLA_HEADS):
        sl = slice(LANES * h, LANES * (h + 1))
        out_ref[0, :, sl] = (_rms(o[:, sl], nw_ref[...]) * _silu(g[:, sl])).astype(out_ref.dtype)


def _gla_post(o_f, o_b, z_g, norm_w, tp, g_col):
    bsz, t_len, vw = o_f.shape
    blk = lambda j: pl.BlockSpec((1, tp, vw), lambda b, t: (b, t, j))
    return pl.pallas_call(
        _gla_post_kernel,
        grid=(bsz, t_len // tp),
        in_specs=[blk(0), blk(0), blk(g_col), pl.BlockSpec(norm_w.shape, lambda b, t: (0, 0))],
        out_specs=blk(0),
        out_shape=jax.ShapeDtypeStruct((bsz, t_len, vw), BF16),
        compiler_params=pltpu.CompilerParams(dimension_semantics=("parallel", "parallel"),
                                             vmem_limit_bytes=VMEM_LIMIT),
        name="gla_post",
    )(o_f, o_b, z_g, norm_w)


def _mix_out_kernel(x_ref, yr_ref, yg_ref, wr_ref, wg_ref, nw_ref, out_ref):
    y = (jnp.dot(yr_ref[...], wr_ref[...], preferred_element_type=F32)
         + jnp.dot(yg_ref[...], wg_ref[...], preferred_element_type=F32))
    out_ref[...] = x_ref[...] + _rms(y, nw_ref[...])


def _mix_out(x2, y_r, y_g, w_or, w_og, nw, tm):
    m, d = x2.shape
    const = lambda i: (0, 0)
    row = lambda w: pl.BlockSpec((tm, w), lambda i: (i, 0))
    return pl.pallas_call(
        _mix_out_kernel,
        grid=(m // tm,),
        in_specs=[row(d), row(y_r.shape[1]), row(y_g.shape[1]),
                  pl.BlockSpec(w_or.shape, const), pl.BlockSpec(w_og.shape, const),
                  pl.BlockSpec((1, d), const)],
        out_specs=row(d),
        out_shape=jax.ShapeDtypeStruct((m, d), F32),
        compiler_params=pltpu.CompilerParams(dimension_semantics=("parallel",),
                                             vmem_limit_bytes=VMEM_LIMIT),
        name="mix_out",
    )(x2, y_r, y_g, w_or, w_og, nw)


def _ffn_kernel(h_ref, npre_ref, npost_ref, wg_ref, wu_ref, wd_ref, out_ref):
    h = h_ref[...]
    hn = _rms(h, npre_ref[...]).astype(BF16)
    gate = jnp.dot(hn, wg_ref[...], preferred_element_type=F32)
    up = jnp.dot(hn, wu_ref[...], preferred_element_type=F32)
    act = (_silu(gate) * up).astype(BF16)
    f = jnp.dot(act, wd_ref[...], preferred_element_type=F32)
    out_ref[...] = h + _rms(f, npost_ref[...])


def _ffn(h, n_pre, n_post, w_gate, w_up, w_down, tm):
    m, d = h.shape
    const = lambda i: (0, 0)
    row = pl.BlockSpec((tm, d), lambda i: (i, 0))
    resident = lambda arr: pl.BlockSpec(arr.shape, const, pipeline_mode=pl.Buffered(1))
    return pl.pallas_call(
        _ffn_kernel,
        grid=(m // tm,),
        in_specs=[row, pl.BlockSpec((1, d), const), pl.BlockSpec((1, d), const),
                  resident(w_gate), resident(w_up), resident(w_down)],
        out_specs=row,
        out_shape=jax.ShapeDtypeStruct((m, d), F32),
        compiler_params=pltpu.CompilerParams(dimension_semantics=("parallel",),
                                             vmem_limit_bytes=VMEM_LIMIT),
        name="ffn",
    )(h, n_pre, n_post, w_gate, w_up, w_down)


def _ple_kernel(h_ref, p_ref, wp_ref, wg_ref, bg_ref, nw_ref, out_ref):
    h = h_ref[...]
    e = jnp.dot(p_ref[...].astype(BF16), wp_ref[...], preferred_element_type=F32)
    gate = _sigmoid(jnp.dot(h.astype(BF16), wg_ref[...], preferred_element_type=F32) + bg_ref[...])
    out_ref[...] = h + _rms(gate * e, nw_ref[...])


def _ple(h, p2, w_proj, w_gate, b_gate, nw, tm):
    m, d = h.shape
    const = lambda i: (0, 0)
    row = lambda w: pl.BlockSpec((tm, w), lambda i: (i, 0))
    return pl.pallas_call(
        _ple_kernel,
        grid=(m // tm,),
        in_specs=[row(d), row(p2.shape[1]), pl.BlockSpec(w_proj.shape, const),
                  pl.BlockSpec(w_gate.shape, const), pl.BlockSpec((1, d), const),
                  pl.BlockSpec((1, d), const)],
        out_specs=row(d),
        out_shape=jax.ShapeDtypeStruct((m, d), F32),
        compiler_params=pltpu.CompilerParams(dimension_semantics=("parallel",),
                                             vmem_limit_bytes=VMEM_LIMIT),
        name="ple",
    )(h, p2, w_proj, w_gate, b_gate, nw)


def _pad_cols(w, n):
    return jnp.pad(w, ((0, 0), (0, n - w.shape[1])))


def _layer(h, p_i, norm_mix_pre, norm_mix_post, norm_ffn_pre, norm_ffn_post, norm_ple,
           w_in, rwkv_mu, rwkv_w0, rwkv_w_up, rwkv_a0, rwkv_a_up, rwkv_g_up,
           rwkv_k_k, rwkv_k_a, rwkv_r_k, rwkv_ln_w, rwkv_ln_b,
           gla_conv, gla_a_up, gla_a_b, gla_norm, w_out,
           ffn_gate, ffn_up, ffn_down, ple_proj, ple_gate, ple_gate_b):
    bsz, t_len, d = h.shape
    m = bsz * t_len
    rw_w = rwkv_w0.shape[1]
    kw = gla_a_b.shape[1]
    vw = w_out.shape[0] - rw_w
    rwkv_in = 3 * rw_w + 2 * RWKV_DECAY_LORA + RWKV_AAA_LORA + RWKV_GATE_LORA
    row = lambda vec: vec.reshape(1, -1)

    lo_w = 3 * rw_w + 2 * RWKV_DECAY_LORA
    a_end = lo_w + RWKV_AAA_LORA

    def rwkv_cols(wmat):
        return jnp.concatenate([wmat[:, :lo_w], _pad_cols(wmat[:, lo_w:a_end], LANES),
                                wmat[:, a_end:rwkv_in]], axis=1)

    w_r = rwkv_cols(w_in[:, :rwkv_in]).astype(BF16)
    mu = rwkv_cols(row(rwkv_mu))
    gla_in = w_in.shape[1] - rwkv_in
    gla_main = gla_in - 2 * GLA_GATE_LORA
    w_g = jnp.concatenate([w_in[:, rwkv_in:rwkv_in + gla_main],
                           _pad_cols(w_in[:, rwkv_in + gla_main:], LANES)], axis=1).astype(BF16)

    zeros = jnp.zeros((RWKV_DECAY_LORA, rw_w), F32)
    w_up = jnp.concatenate([jnp.concatenate([rwkv_w_up[0], zeros], axis=1),
                            jnp.concatenate([zeros, rwkv_w_up[1]], axis=1)], axis=0).astype(BF16)
    a_up = jnp.pad(rwkv_a_up, ((0, LANES - RWKV_AAA_LORA), (0, 0))).astype(BF16)
    g_up = rwkv_g_up.astype(BF16)
    gz = jnp.zeros((GLA_GATE_LORA, kw), F32)
    gla_up = jnp.concatenate([jnp.concatenate([gla_a_up[0], gz], axis=1),
                              jnp.concatenate([gz, gla_a_up[1]], axis=1)], axis=0)
    gla_up = jnp.pad(gla_up, ((0, LANES - 2 * GLA_GATE_LORA), (0, 0))).astype(BF16)

    tm = 256
    tp = 256
    x2 = h.reshape(m, d)
    z_r, z_g = _inproj(x2, row(norm_mix_pre), w_r, w_g, tm)
    z_r = z_r.reshape(bsz, t_len, -1)
    z_g = z_g.reshape(bsz, t_len, -1)

    rw = _rwkv_prep(z_r, mu, rwkv_w0.reshape(1, -1), w_up, row(rwkv_a0), a_up, g_up,
                    row(rwkv_k_k), row(rwkv_k_a), tp, rw_w)
    y_f = _rwkv_scan(rw, rw_w, reverse=False)
    y_b = _rwkv_scan(rw, rw_w, reverse=True)
    y_r = _rwkv_post(y_f, y_b, rw, row(rwkv_r_k), row(rwkv_ln_w), row(rwkv_ln_b), tp)

    gl = _gla_prep(z_g, gla_conv.reshape(gla_conv.shape[0], -1), gla_up, gla_a_b.reshape(1, -1), tp, kw, vw)
    o_f = _gla_scan(gl, kw, vw, reverse=False)
    o_b = _gla_scan(gl, kw, vw, reverse=True)
    y_g = _gla_post(o_f, o_b, z_g, row(gla_norm), tp, (2 * kw + vw) // vw)

    w_o = w_out.astype(BF16)
    h1 = _mix_out(x2, y_r.reshape(m, rw_w), y_g.reshape(m, vw), w_o[:rw_w], w_o[rw_w:],
                  row(norm_mix_post), tm)
    h2 = _ffn(h1, row(norm_ffn_pre), row(norm_ffn_post), ffn_gate.astype(BF16), ffn_up.astype(BF16),
              ffn_down.astype(BF16), tm)
    h3 = _ple(h2, p_i.reshape(m, -1), ple_proj.astype(BF16), ple_gate.astype(BF16), row(ple_gate_b),
              row(norm_ple), tm)
    return h3.reshape(bsz, t_len, d)


def kernel(x, p, norm_mix_pre, norm_mix_post, norm_ffn_pre, norm_ffn_post, norm_ple, w_in, rwkv_mu, rwkv_w0, rwkv_w_up, rwkv_a0, rwkv_a_up, rwkv_g_up, rwkv_k_k, rwkv_k_a, rwkv_r_k, rwkv_ln_w, rwkv_ln_b, gla_conv, gla_a_up, gla_a_b, gla_norm, w_out, ffn_gate, ffn_up, ffn_down, ple_proj, ple_gate, ple_gate_b):
    params = (norm_mix_pre, norm_mix_post, norm_ffn_pre, norm_ffn_post, norm_ple, w_in, rwkv_mu,
              rwkv_w0, rwkv_w_up, rwkv_a0, rwkv_a_up, rwkv_g_up, rwkv_k_k, rwkv_k_a, rwkv_r_k,
              rwkv_ln_w, rwkv_ln_b, gla_conv, gla_a_up, gla_a_b, gla_norm, w_out, ffn_gate, ffn_up,
              ffn_down, ple_proj, ple_gate, ple_gate_b)
    h = x
    for i in range(p.shape[0]):
        h = _layer(h, p[i], *(w[i] for w in params))
    return h
```

```python
import functools

import jax
import jax.numpy as jnp
from jax import lax
from jax.experimental import pallas as pl
from jax.experimental.pallas import tpu as pltpu

F32 = jnp.float32
BF16 = jnp.bfloat16

NORM_EPS = 1e-6
RWKV_GN_EPS = 64e-5
RWKV_HEAD_DIM = 64
RWKV_DECAY_LORA = 64
RWKV_AAA_LORA = 64
RWKV_GATE_LORA = 128
GLA_HEADS = 4
GLA_GATE_LORA = 16
GLA_GATE_TEMP = 16.0
CHUNK = 64
LANES = 128
VMEM_LIMIT = 56 * 1024 * 1024


def _mm(a, b):
    return jnp.dot(a.astype(BF16), b.astype(BF16), preferred_element_type=F32)


def _mm_nt(a, b):
    return lax.dot_general(a.astype(BF16), b.astype(BF16), (((1,), (1,)), ((), ())),
                           preferred_element_type=F32)


def _mm_tn(a, b):
    return lax.dot_general(a.astype(BF16), b.astype(BF16), (((0,), (0,)), ((), ())),
                           preferred_element_type=F32)


def _split3(x):
    hi = x.astype(BF16)
    r1 = x - hi.astype(F32)
    mid = r1.astype(BF16)
    lo = (r1 - mid.astype(F32)).astype(BF16)
    return hi, mid, lo


def _mm_exact_lhs(a_bf16, x):
    hi, mid, lo = _split3(x)
    d = functools.partial(jnp.dot, preferred_element_type=F32)
    return d(a_bf16, hi) + d(a_bf16, mid) + d(a_bf16, lo)


def _mm_exact_rhs(x, b_bf16):
    hi, mid, lo = _split3(x)
    d = functools.partial(jnp.dot, preferred_element_type=F32)
    return d(hi, b_bf16) + d(mid, b_bf16) + d(lo, b_bf16)


def _rms(x, w):
    ms = jnp.mean(x * x, axis=-1, keepdims=True)
    return x * lax.rsqrt(ms + NORM_EPS) * w


def _softplus(u):
    return jnp.maximum(u, 0.0) + jnp.log(1.0 + jnp.exp(-jnp.abs(u)))


def _sigmoid(u):
    return 1.0 / (1.0 + jnp.exp(-u))


def _silu(u):
    return u * _sigmoid(u)


def _shifted(z, prev_row, next_row):
    tp = z.shape[0]
    row = lax.broadcasted_iota(jnp.int32, (tp, 1), 0)
    z_dn = jnp.where(row == 0, prev_row, pltpu.roll(z, 1, 0))
    z_up = jnp.where(row == tp - 1, next_row, pltpu.roll(z, tp - 1, 0))
    return z_dn, z_up


def _halo_rows(zp_ref, zn_ref):
    t = pl.program_id(1)
    nt = pl.num_programs(1)
    prev_row = jnp.where(t > 0, zp_ref[0, 7:8, :], 0.0)
    next_row = jnp.where(t < nt - 1, zn_ref[0, 0:1, :], 0.0)
    return prev_row, next_row


def _block_ones(width, seg):
    r = lax.broadcasted_iota(jnp.int32, (width, width), 0) // seg
    c = lax.broadcasted_iota(jnp.int32, (width, width), 1) // seg
    return (r == c).astype(BF16)


def _pair_stack(z, head0_lanes):
    zero = jnp.zeros_like(z)
    return jnp.concatenate([jnp.where(head0_lanes, z, zero), jnp.where(head0_lanes, zero, z)], axis=0)


def _pair_masks(reverse):
    n = 2 * CHUNK
    r = lax.broadcasted_iota(jnp.int32, (n, n), 0)
    c = lax.broadcasted_iota(jnp.int32, (n, n), 1)
    same = (r // CHUNK) == (c // CHUNK)
    tt = r % CHUNK
    ss = c % CHUNK
    if reverse:
        incl = same & (ss >= tt)
        strict = same & (ss > tt)
    else:
        incl = same & (ss <= tt)
        strict = same & (ss < tt)
    return incl, strict


def _chunk_tri(reverse):
    r = lax.broadcasted_iota(jnp.int32, (CHUNK, CHUNK), 0)
    c = lax.broadcasted_iota(jnp.int32, (CHUNK, CHUNK), 1)
    m = (c >= r) if reverse else (c <= r)
    return m.astype(BF16)


def _inproj_kernel(x_ref, nw_ref, wr_ref, wg_ref, zr_ref, zg_ref):
    xn = _rms(x_ref[...], nw_ref[...]).astype(BF16)
    zr_ref[...] = jnp.dot(xn, wr_ref[...], preferred_element_type=F32)
    zg_ref[...] = jnp.dot(xn, wg_ref[...], preferred_element_type=F32)


def _inproj(x2, nw, w_r, w_g, tm):
    m, d = x2.shape
    nr, ng = w_r.shape[1], w_g.shape[1]
    const = lambda i: (0, 0)
    return pl.pallas_call(
        _inproj_kernel,
        grid=(m // tm,),
        in_specs=[pl.BlockSpec((tm, d), lambda i: (i, 0)),
                  pl.BlockSpec((1, d), const),
                  pl.BlockSpec((d, nr), const),
                  pl.BlockSpec((d, ng), const)],
        out_specs=[pl.BlockSpec((tm, nr), lambda i: (i, 0)),
                   pl.BlockSpec((tm, ng), lambda i: (i, 0))],
        out_shape=[jax.ShapeDtypeStruct((m, nr), F32), jax.ShapeDtypeStruct((m, ng), F32)],
        compiler_params=pltpu.CompilerParams(dimension_semantics=("parallel",),
                                             vmem_limit_bytes=VMEM_LIMIT),
        name="inproj",
    )(x2, nw, w_r, w_g)


def _rwkv_prep_kernel(z_ref, zp_ref, zn_ref, mu_ref, w0_ref, wup_ref, a0_ref, aup_ref, gup_ref,
                      kk_ref, ka_ref, out_ref, *, width):
    w = width
    z = z_ref[0]
    prev_row, next_row = _halo_rows(zp_ref, zn_ref)
    z_dn, z_up = _shifted(z, prev_row, next_row)
    zs = z + mu_ref[...] * (0.5 * (z_dn + z_up) - z)
    r = zs[:, 0:w]
    k = zs[:, w:2 * w]
    v = zs[:, 2 * w:3 * w]
    w_lo = zs[:, 3 * w:3 * w + LANES]
    a_lo = zs[:, 3 * w + LANES:3 * w + 2 * LANES]
    g_lo = zs[:, 3 * w + 2 * LANES:3 * w + 3 * LANES]
    w_log = -_softplus(-(w0_ref[...] + _mm(jnp.tanh(w_lo), wup_ref[...]))) - 0.5
    lw = -jnp.exp(w_log)
    a = _sigmoid(a0_ref[...] + _mm(a_lo, aup_ref[...]))
    g = _mm(_sigmoid(g_lo), gup_ref[...])
    kk = k * kk_ref[...]
    ss = _mm_exact_rhs(kk * kk, _block_ones(w, RWKV_HEAD_DIM))
    kk = kk * lax.rsqrt(jnp.maximum(ss, 1e-24))
    k = k * (1.0 + (a - 1.0) * ka_ref[...])
    out_ref[0, :, 0:w] = r
    out_ref[0, :, w:2 * w] = k
    out_ref[0, :, 2 * w:3 * w] = v
    out_ref[0, :, 3 * w:4 * w] = kk
    out_ref[0, :, 4 * w:5 * w] = kk * a
    out_ref[0, :, 5 * w:7 * w] = lw
    out_ref[0, :, 7 * w:8 * w] = g


def _halo_specs(tp, width, t_len):
    nb8 = t_len // 8
    per = tp // 8
    main = pl.BlockSpec((1, tp, width), lambda b, t: (b, t, 0))
    prev = pl.BlockSpec((1, 8, width), lambda b, t: (b, jnp.maximum(t * per - 1, 0), 0))
    nxt = pl.BlockSpec((1, 8, width), lambda b, t: (b, jnp.minimum((t + 1) * per, nb8 - 1), 0))
    return [main, prev, nxt]


def _rwkv_prep(z_r, mu, w0, w_up, a0, a_up, g_up, k_k, k_a, tp, width):
    bsz, t_len, zw = z_r.shape
    const2 = lambda b, t: (0, 0)
    full = lambda arr: pl.BlockSpec(arr.shape, const2)
    return pl.pallas_call(
        functools.partial(_rwkv_prep_kernel, width=width),
        grid=(bsz, t_len // tp),
        in_specs=_halo_specs(tp, zw, t_len) + [full(mu), full(w0), full(w_up), full(a0), full(a_up),
                                               full(g_up), full(k_k), full(k_a)],
        out_specs=pl.BlockSpec((1, tp, 8 * width), lambda b, t: (b, t, 0)),
        out_shape=jax.ShapeDtypeStruct((bsz, t_len, 8 * width), F32),
        compiler_params=pltpu.CompilerParams(dimension_semantics=("parallel", "parallel"),
                                             vmem_limit_bytes=VMEM_LIMIT),
        name="rwkv_prep",
    )(z_r, z_r, z_r, mu, w0, w_up, a0, a_up, g_up, k_k, k_a)


def _neumann_inverses(mats):
    n = mats[0].shape[0]
    eye = (lax.broadcasted_iota(jnp.int32, (n, n), 0) == lax.broadcasted_iota(jnp.int32, (n, n), 1))
    ts = [a + eye.astype(F32) for a in mats]
    ps = [_mm(a, a) for a in mats]
    levels = CHUNK.bit_length() - 1
    for level in range(2, levels):
        prods = [_mm(p, jnp.concatenate([t, p], axis=1)) for t, p in zip(ts, ps)]
        ts = [t + pr[:, :n] for t, pr in zip(ts, prods)]
        ps = [pr[:, n:] for pr in prods]
    return [t + _mm(p, t) for t, p in zip(ts, ps)]


def _delta_chunk_operands(refs, reverse):
    r_ref, k_ref, v_ref, kk_ref, b_ref, lw_ref = refs
    lw = lw_ref[0]
    g = _mm_exact_lhs(_chunk_tri(reverse), lw)
    total = jnp.sum(lw, axis=0, keepdims=True)
    einv = jnp.exp(-g)
    ec = jnp.exp(total - g)
    kk = kk_ref[0]
    bb = b_ref[0]
    kx = k_ref[0]
    full = dict(a=-kk * jnp.exp(g - lw), r=r_ref[0] * jnp.exp(g), b=bb * einv, k=kx * einv,
                v=v_ref[0], bh=bb * ec, kh=kx * ec)
    head0 = lax.broadcasted_iota(jnp.int32, (CHUNK, LANES), 1) < RWKV_HEAD_DIM
    incl, strict = _pair_masks(reverse)
    gam = jnp.exp(total)
    probs = []
    for p in range(lw.shape[1] // LANES):
        sl = slice(LANES * p, LANES * (p + 1))
        prob = {name: _pair_stack(z[:, sl], head0) for name, z in full.items()}
        prob.update(incl=incl, strict=strict, gam=gam[:, sl], lanes=sl)
        probs.append(prob)
    return probs


def _rwkv_scan_kernel(*refs):
    in_f, in_b = refs[0:6], refs[6:12]
    yf_ref, yb_ref, ht_ref = refs[12:15]

    @pl.when(pl.program_id(1) == 0)
    def _():
        ht_ref[...] = jnp.zeros_like(ht_ref)

    c = CHUNK
    n = 2 * c
    zero = jnp.zeros((n, n), F32)
    probs = _delta_chunk_operands(in_f, False) + _delta_chunk_operands(in_b, True)
    n_pairs = len(probs) // 2
    outs = [yf_ref] * n_pairs + [yb_ref] * n_pairs
    gmats = [_mm_nt(jnp.concatenate([q["a"], q["r"]], axis=0), jnp.concatenate([q["b"], q["k"]], axis=0))
             for q in probs]
    a_ab = [jnp.where(q["strict"], gm[:n, :n], zero) for q, gm in zip(probs, gmats)]
    a_ak = [jnp.where(q["strict"], gm[:n, n:], zero) for q, gm in zip(probs, gmats)]
    a_rb = [jnp.where(q["incl"], gm[n:, :n], zero) for q, gm in zip(probs, gmats)]
    a_rk = [jnp.where(q["incl"], gm[n:, n:], zero) for q, gm in zip(probs, gmats)]
    akv = [_mm(a, q["v"]) for a, q in zip(a_ak, probs)]
    y0 = [_mm(a, q["v"]) for a, q in zip(a_rk, probs)]
    kv = [_mm_tn(q["v"], q["kh"]) for q in probs]
    tinv = _neumann_inverses(a_ab)
    w12 = [_mm(t, jnp.concatenate([q["a"], x], axis=1)) for t, q, x in zip(tinv, probs, akv)]
    rw = [_mm(a, w) for a, w in zip(a_rb, w12)]
    m12 = [_mm_tn(w, q["bh"]) for w, q in zip(w12, probs)]
    hts = [ht_ref[i] for i in range(len(probs))]
    ys = [_mm_nt(q["r"] + x[:, :n], h) + x[:, n:] + y for q, x, h, y in zip(probs, rw, hts, y0)]
    hm = [_mm(h, m[:n]) for h, m in zip(hts, m12)]
    for i, q in enumerate(probs):
        outs[i][0, :, q["lanes"]] = ys[i][:c] + ys[i][c:]
        ht_ref[i] = hts[i] * q["gam"] + hm[i] + m12[i][n:] + kv[i]


def _rwkv_scan(rw, width):
    bsz, t_len, _ = rw.shape
    nc = t_len // CHUNK
    fwd = lambda j: pl.BlockSpec((1, CHUNK, width), lambda b, t: (b, t, j))
    bwd = lambda j: pl.BlockSpec((1, CHUNK, width), lambda b, t: (b, nc - 1 - t, j))
    n_pairs = width // LANES
    y_shape = jax.ShapeDtypeStruct((bsz, t_len, width), F32)
    return pl.pallas_call(
        _rwkv_scan_kernel,
        grid=(bsz, nc),
        in_specs=[fwd(j) for j in (0, 1, 2, 3, 4, 5)] + [bwd(j) for j in (0, 1, 2, 3, 4, 6)],
        out_specs=[fwd(0), bwd(0)],
        out_shape=[y_shape, y_shape],
        scratch_shapes=[pltpu.VMEM((2 * n_pairs, 2 * CHUNK, 2 * CHUNK), F32)],
        compiler_params=pltpu.CompilerParams(dimension_semantics=("parallel", "arbitrary"),
                                             vmem_limit_bytes=VMEM_LIMIT),
        name="rwkv_scan",
    )(*([rw] * 12))


def _rwkv_post_kernel(yf_ref, yb_ref, r_ref, k_ref, v_ref, g_ref, rk_ref, lnw_ref, lnb_ref, out_ref):
    w = yf_ref.shape[-1]
    ones = _block_ones(w, RWKV_HEAD_DIM)
    inv_n = 1.0 / RWKV_HEAD_DIM
    y = yf_ref[0] + yb_ref[0]
    mu = _mm_exact_rhs(y, ones) * inv_n
    yc = y - mu
    var = _mm_exact_rhs(yc * yc, ones) * inv_n
    yn = yc * lax.rsqrt(var + RWKV_GN_EPS) * lnw_ref[...] + lnb_ref[...]
    bonus = _mm_exact_rhs(r_ref[0] * k_ref[0] * rk_ref[...], ones) * v_ref[0]
    out_ref[0] = ((yn + bonus) * g_ref[0]).astype(out_ref.dtype)


def _rwkv_post(y_f, y_b, rw, r_k, ln_w, ln_b, tp):
    bsz, t_len, width = y_f.shape
    blk = lambda j: pl.BlockSpec((1, tp, width), lambda b, t: (b, t, j))
    vec = pl.BlockSpec((1, width), lambda b, t: (0, 0))
    return pl.pallas_call(
        _rwkv_post_kernel,
        grid=(bsz, t_len // tp),
        in_specs=[blk(0), blk(0), blk(0), blk(1), blk(2), blk(7), vec, vec, vec],
        out_specs=blk(0),
        out_shape=jax.ShapeDtypeStruct((bsz, t_len, width), BF16),
        compiler_params=pltpu.CompilerParams(dimension_semantics=("parallel", "parallel"),
                                             vmem_limit_bytes=VMEM_LIMIT),
        name="rwkv_post",
    )(y_f, y_b, rw, rw, rw, rw, r_k, ln_w, ln_b)


def _gla_prep_kernel(z_ref, zp_ref, zn_ref, cw_ref, aup_ref, ab_ref, out_ref, *, kw, vw):
    qkv_w = 2 * kw + vw
    z = z_ref[0]
    prev_row, next_row = _halo_rows(zp_ref, zn_ref)
    x = z[:, :qkv_w]
    x_dn, x_up = _shifted(x, prev_row[:, :qkv_w], next_row[:, :qkv_w])
    qkv = _silu(cw_ref[0:1, :] * x_dn + cw_ref[1:2, :] * x + cw_ref[2:3, :] * x_up)
    a_lo = z[:, qkv_w + vw:qkv_w + vw + LANES]
    logit = _mm(a_lo, aup_ref[...]) + ab_ref[...]
    log_a = -_softplus(-logit) * (1.0 / GLA_GATE_TEMP)
    dk = kw // GLA_HEADS
    out_ref[0, :, 0:kw] = qkv[:, 0:kw] * (dk ** -0.5)
    out_ref[0, :, kw:qkv_w] = qkv[:, kw:qkv_w]
    out_ref[0, :, qkv_w:qkv_w + 2 * kw] = log_a


def _gla_prep(z_g, conv_w, a_up, a_b, tp, kw, vw):
    bsz, t_len, zw = z_g.shape
    ow = 4 * kw + vw
    const2 = lambda b, t: (0, 0)
    full = lambda arr: pl.BlockSpec(arr.shape, const2)
    return pl.pallas_call(
        functools.partial(_gla_prep_kernel, kw=kw, vw=vw),
        grid=(bsz, t_len // tp),
        in_specs=_halo_specs(tp, zw, t_len) + [full(conv_w), full(a_up), full(a_b)],
        out_specs=pl.BlockSpec((1, tp, ow), lambda b, t: (b, t, 0)),
        out_shape=jax.ShapeDtypeStruct((bsz, t_len, ow), F32),
        compiler_params=pltpu.CompilerParams(dimension_semantics=("parallel", "parallel"),
                                             vmem_limit_bytes=VMEM_LIMIT),
        name="gla_prep",
    )(z_g, z_g, z_g, conv_w, a_up, a_b)


def _gla_chunk_operands(refs, reverse, slot0):
    q_ref, k_ref, v_ref, la_ref = refs
    c = CHUNK
    la = la_ref[0]
    bc = _mm_exact_lhs(_chunk_tri(reverse), la)
    total = jnp.sum(la, axis=0, keepdims=True)
    q_t = q_ref[0] * jnp.exp(bc)
    k_t = k_ref[0] * jnp.exp(-bc)
    k_h = k_ref[0] * jnp.exp(total - bc)
    gam = jnp.exp(total)
    v = v_ref[0]
    r = lax.broadcasted_iota(jnp.int32, (c, c), 0)
    cc = lax.broadcasted_iota(jnp.int32, (c, c), 1)
    causal = (cc >= r) if reverse else (cc <= r)
    head0 = lax.broadcasted_iota(jnp.int32, (c, LANES), 1) < (LANES // 2)
    zero = jnp.zeros((c, LANES), F32)
    probs = []
    for p in range(la.shape[1] // LANES):
        sl = slice(LANES * p, LANES * (p + 1))
        for h in range(2):
            hm = head0 if h == 0 else jnp.logical_not(head0)
            vsl = slice(LANES * (2 * p + h), LANES * (2 * p + h + 1))
            probs.append(dict(q=jnp.where(hm, q_t[:, sl], zero), kt=k_t[:, sl],
                              kh=jnp.where(hm, k_h[:, sl], zero), v=v[:, vsl], causal=causal,
                              gam=gam[:, sl], slot=slot0 + p, lanes=vsl))
    return probs


def _gla_scan_kernel(*refs):
    in_f, in_b = refs[0:4], refs[4:8]
    of_ref, ob_ref, st_ref = refs[8:11]

    @pl.when(pl.program_id(1) == 0)
    def _():
        st_ref[...] = jnp.zeros_like(st_ref)

    n_slots = st_ref.shape[0]
    probs = _gla_chunk_operands(in_f, False, 0) + _gla_chunk_operands(in_b, True, n_slots // 2)
    outs = [of_ref] * (len(probs) // 2) + [ob_ref] * (len(probs) // 2)
    sts = [st_ref[s] for s in range(n_slots)]
    scores = [jnp.where(q["causal"], _mm_nt(q["q"], q["kt"]), 0.0) for q in probs]
    intra = [_mm(s, q["v"]) for s, q in zip(scores, probs)]
    inter = [_mm_nt(q["q"], sts[q["slot"]]) for q in probs]
    upd = [_mm_tn(q["v"], q["kh"]) for q in probs]
    for i, q in enumerate(probs):
        outs[i][0, :, q["lanes"]] = intra[i] + inter[i]
    for s in range(n_slots):
        mine = [i for i, q in enumerate(probs) if q["slot"] == s]
        st_ref[s] = sts[s] * probs[mine[0]]["gam"] + upd[mine[0]] + upd[mine[1]]


def _gla_scan(gl, kw, vw):
    bsz, t_len, _ = gl.shape
    nc = t_len // CHUNK
    fwd = lambda w, j: pl.BlockSpec((1, CHUNK, w), lambda b, t: (b, t, j))
    bwd = lambda w, j: pl.BlockSpec((1, CHUNK, w), lambda b, t: (b, nc - 1 - t, j))
    la_col = (2 * kw + vw) // kw
    o_shape = jax.ShapeDtypeStruct((bsz, t_len, vw), F32)
    specs = lambda blk, la: [blk(kw, 0), blk(kw, 1), blk(vw, (2 * kw) // vw), blk(kw, la)]
    return pl.pallas_call(
        _gla_scan_kernel,
        grid=(bsz, nc),
        in_specs=specs(fwd, la_col) + specs(bwd, la_col + 1),
        out_specs=[fwd(vw, 0), bwd(vw, 0)],
        out_shape=[o_shape, o_shape],
        scratch_shapes=[pltpu.VMEM((2 * (kw // LANES), LANES, LANES), F32)],
        compiler_params=pltpu.CompilerParams(dimension_semantics=("parallel", "arbitrary"),
                                             vmem_limit_bytes=VMEM_LIMIT),
        name="gla_scan",
    )(*([gl] * 8))


def _gla_post_kernel(of_ref, ob_ref, g_ref, nw_ref, out_ref):
    o = of_ref[0] + ob_ref[0]
    g = g_ref[0]
    for h in range(GLA_HEADS):
        sl = slice(LANES * h, LANES * (h + 1))
        out_ref[0, :, sl] = (_rms(o[:, sl], nw_ref[...]) * _silu(g[:, sl])).astype(out_ref.dtype)


def _gla_post(o_f, o_b, z_g, norm_w, tp, g_col):
    bsz, t_len, vw = o_f.shape
    blk = lambda j: pl.BlockSpec((1, tp, vw), lambda b, t: (b, t, j))
    return pl.pallas_call(
        _gla_post_kernel,
        grid=(bsz, t_len // tp),
        in_specs=[blk(0), blk(0), blk(g_col), pl.BlockSpec(norm_w.shape, lambda b, t: (0, 0))],
        out_specs=blk(0),
        out_shape=jax.ShapeDtypeStruct((bsz, t_len, vw), BF16),
        compiler_params=pltpu.CompilerParams(dimension_semantics=("parallel", "parallel"),
                                             vmem_limit_bytes=VMEM_LIMIT),
        name="gla_post",
    )(o_f, o_b, z_g, norm_w)


def _mix_out_kernel(x_ref, yr_ref, yg_ref, wr_ref, wg_ref, nw_ref, out_ref):
    y = (jnp.dot(yr_ref[...], wr_ref[...], preferred_element_type=F32)
         + jnp.dot(yg_ref[...], wg_ref[...], preferred_element_type=F32))
    out_ref[...] = x_ref[...] + _rms(y, nw_ref[...])


def _mix_out(x2, y_r, y_g, w_or, w_og, nw, tm):
    m, d = x2.shape
    const = lambda i: (0, 0)
    row = lambda w: pl.BlockSpec((tm, w), lambda i: (i, 0))
    return pl.pallas_call(
        _mix_out_kernel,
        grid=(m // tm,),
        in_specs=[row(d), row(y_r.shape[1]), row(y_g.shape[1]),
                  pl.BlockSpec(w_or.shape, const), pl.BlockSpec(w_og.shape, const),
                  pl.BlockSpec((1, d), const)],
        out_specs=row(d),
        out_shape=jax.ShapeDtypeStruct((m, d), F32),
        compiler_params=pltpu.CompilerParams(dimension_semantics=("parallel",),
                                             vmem_limit_bytes=VMEM_LIMIT),
        name="mix_out",
    )(x2, y_r, y_g, w_or, w_og, nw)


def _ffn_kernel(h_ref, npre_ref, npost_ref, wg_ref, wu_ref, wd_ref, out_ref):
    h = h_ref[...]
    hn = _rms(h, npre_ref[...]).astype(BF16)
    gate = jnp.dot(hn, wg_ref[...], preferred_element_type=F32)
    up = jnp.dot(hn, wu_ref[...], preferred_element_type=F32)
    act = (_silu(gate) * up).astype(BF16)
    f = jnp.dot(act, wd_ref[...], preferred_element_type=F32)
    out_ref[...] = h + _rms(f, npost_ref[...])


def _ffn(h, n_pre, n_post, w_gate, w_up, w_down, tm):
    m, d = h.shape
    const = lambda i: (0, 0)
    row = pl.BlockSpec((tm, d), lambda i: (i, 0))
    resident = lambda arr: pl.BlockSpec(arr.shape, const, pipeline_mode=pl.Buffered(1))
    return pl.pallas_call(
        _ffn_kernel,
        grid=(m // tm,),
        in_specs=[row, pl.BlockSpec((1, d), const), pl.BlockSpec((1, d), const),
                  resident(w_gate), resident(w_up), resident(w_down)],
        out_specs=row,
        out_shape=jax.ShapeDtypeStruct((m, d), F32),
        compiler_params=pltpu.CompilerParams(dimension_semantics=("parallel",),
                                             vmem_limit_bytes=VMEM_LIMIT),
        name="ffn",
    )(h, n_pre, n_post, w_gate, w_up, w_down)


def _ple_kernel(h_ref, p_ref, wp_ref, wg_ref, bg_ref, nw_ref, out_ref):
    h = h_ref[...]
    e = jnp.dot(p_ref[...].astype(BF16), wp_ref[...], preferred_element_type=F32)
    gate = _sigmoid(jnp.dot(h.astype(BF16), wg_ref[...], preferred_element_type=F32) + bg_ref[...])
    out_ref[...] = h + _rms(gate * e, nw_ref[...])


def _ple(h, p2, w_proj, w_gate, b_gate, nw, tm):
    m, d = h.shape
    const = lambda i: (0, 0)
    row = lambda w: pl.BlockSpec((tm, w), lambda i: (i, 0))
    return pl.pallas_call(
        _ple_kernel,
        grid=(m // tm,),
        in_specs=[row(d), row(p2.shape[1]), pl.BlockSpec(w_proj.shape, const),
                  pl.BlockSpec(w_gate.shape, const), pl.BlockSpec((1, d), const),
                  pl.BlockSpec((1, d), const)],
        out_specs=row(d),
        out_shape=jax.ShapeDtypeStruct((m, d), F32),
        compiler_params=pltpu.CompilerParams(dimension_semantics=("parallel",),
                                             vmem_limit_bytes=VMEM_LIMIT),
        name="ple",
    )(h, p2, w_proj, w_gate, b_gate, nw)


def _pad_cols(w, n):
    return jnp.pad(w, ((0, 0), (0, n - w.shape[1])))


def _layer(h, p_i, norm_mix_pre, norm_mix_post, norm_ffn_pre, norm_ffn_post, norm_ple,
           w_in, rwkv_mu, rwkv_w0, rwkv_w_up, rwkv_a0, rwkv_a_up, rwkv_g_up,
           rwkv_k_k, rwkv_k_a, rwkv_r_k, rwkv_ln_w, rwkv_ln_b,
           gla_conv, gla_a_up, gla_a_b, gla_norm, w_out,
           ffn_gate, ffn_up, ffn_down, ple_proj, ple_gate, ple_gate_b):
    bsz, t_len, d = h.shape
    m = bsz * t_len
    rw_w = rwkv_w0.shape[1]
    kw = gla_a_b.shape[1]
    vw = w_out.shape[0] - rw_w
    rwkv_in = 3 * rw_w + 2 * RWKV_DECAY_LORA + RWKV_AAA_LORA + RWKV_GATE_LORA
    row = lambda vec: vec.reshape(1, -1)

    lo_w = 3 * rw_w + 2 * RWKV_DECAY_LORA
    a_end = lo_w + RWKV_AAA_LORA

    def rwkv_cols(wmat):
        return jnp.concatenate([wmat[:, :lo_w], _pad_cols(wmat[:, lo_w:a_end], LANES),
                                wmat[:, a_end:rwkv_in]], axis=1)

    w_r = rwkv_cols(w_in[:, :rwkv_in]).astype(BF16)
    mu = rwkv_cols(row(rwkv_mu))
    gla_in = w_in.shape[1] - rwkv_in
    gla_main = gla_in - 2 * GLA_GATE_LORA
    w_g = jnp.concatenate([w_in[:, rwkv_in:rwkv_in + gla_main],
                           _pad_cols(w_in[:, rwkv_in + gla_main:], LANES)], axis=1).astype(BF16)

    zeros = jnp.zeros((RWKV_DECAY_LORA, rw_w), F32)
    w_up = jnp.concatenate([jnp.concatenate([rwkv_w_up[0], zeros], axis=1),
                            jnp.concatenate([zeros, rwkv_w_up[1]], axis=1)], axis=0).astype(BF16)
    a_up = jnp.pad(rwkv_a_up, ((0, LANES - RWKV_AAA_LORA), (0, 0))).astype(BF16)
    g_up = rwkv_g_up.astype(BF16)
    gz = jnp.zeros((GLA_GATE_LORA, kw), F32)
    gla_up = jnp.concatenate([jnp.concatenate([gla_a_up[0], gz], axis=1),
                              jnp.concatenate([gz, gla_a_up[1]], axis=1)], axis=0)
    gla_up = jnp.pad(gla_up, ((0, LANES - 2 * GLA_GATE_LORA), (0, 0))).astype(BF16)

    tm = 256
    tp = 256
    x2 = h.reshape(m, d)
    z_r, z_g = _inproj(x2, row(norm_mix_pre), w_r, w_g, tm)
    z_r = z_r.reshape(bsz, t_len, -1)
    z_g = z_g.reshape(bsz, t_len, -1)

    rw = _rwkv_prep(z_r, mu, rwkv_w0.reshape(1, -1), w_up, row(rwkv_a0), a_up, g_up,
                    row(rwkv_k_k), row(rwkv_k_a), tp, rw_w)
    y_f, y_b = _rwkv_scan(rw, rw_w)
    y_r = _rwkv_post(y_f, y_b, rw, row(rwkv_r_k), row(rwkv_ln_w), row(rwkv_ln_b), tp)

    gl = _gla_prep(z_g, gla_conv.reshape(gla_conv.shape[0], -1), gla_up, gla_a_b.reshape(1, -1), tp, kw, vw)
    o_f, o_b = _gla_scan(gl, kw, vw)
    y_g = _gla_post(o_f, o_b, z_g, row(gla_norm), tp, (2 * kw + vw) // vw)

    w_o = w_out.astype(BF16)
    h1 = _mix_out(x2, y_r.reshape(m, rw_w), y_g.reshape(m, vw), w_o[:rw_w], w_o[rw_w:],
                  row(norm_mix_post), tm)
    h2 = _ffn(h1, row(norm_ffn_pre), row(norm_ffn_post), ffn_gate.astype(BF16), ffn_up.astype(BF16),
              ffn_down.astype(BF16), tm)
    h3 = _ple(h2, p_i.reshape(m, -1), ple_proj.astype(BF16), ple_gate.astype(BF16), row(ple_gate_b),
              row(norm_ple), tm)
    return h3.reshape(bsz, t_len, d)


def kernel(x, p, norm_mix_pre, norm_mix_post, norm_ffn_pre, norm_ffn_post, norm_ple, w_in, rwkv_mu, rwkv_w0, rwkv_w_up, rwkv_a0, rwkv_a_up, rwkv_g_up, rwkv_k_k, rwkv_k_a, rwkv_r_k, rwkv_ln_w, rwkv_ln_b, gla_conv, gla_a_up, gla_a_b, gla_norm, w_out, ffn_gate, ffn_up, ffn_down, ple_proj, ple_gate, ple_gate_b):
    params = (norm_mix_pre, norm_mix_post, norm_ffn_pre, norm_ffn_post, norm_ple, w_in, rwkv_mu,
              rwkv_w0, rwkv_w_up, rwkv_a0, rwkv_a_up, rwkv_g_up, rwkv_k_k, rwkv_k_a, rwkv_r_k,
              rwkv_ln_w, rwkv_ln_b, gla_conv, gla_a_up, gla_a_b, gla_norm, w_out, ffn_gate, ffn_up,
              ffn_down, ple_proj, ple_gate, ple_gate_b)
    h = x
    for i in range(p.shape[0]):
        h = _layer(h, p[i], *(w[i] for w in params))
    return h
```

```python
import functools

import jax
import jax.numpy as jnp
from jax import lax
from jax.experimental import pallas as pl
from jax.experimental.pallas import tpu as pltpu

F32 = jnp.float32
BF16 = jnp.bfloat16

NORM_EPS = 1e-6
RWKV_GN_EPS = 64e-5
RWKV_HEAD_DIM = 64
RWKV_DECAY_LORA = 64
RWKV_AAA_LORA = 64
RWKV_GATE_LORA = 128
GLA_HEADS = 4
GLA_GATE_LORA = 16
GLA_GATE_TEMP = 16.0
CHUNK = 64
GLA_BLOCK_CHUNKS = 4
LANES = 128
VMEM_LIMIT = 56 * 1024 * 1024
ROW_TILE = 256


def _mm(a, b):
    return jnp.dot(a.astype(BF16), b.astype(BF16), preferred_element_type=F32)


def _mm_nt(a, b):
    return lax.dot_general(a.astype(BF16), b.astype(BF16), (((1,), (1,)), ((), ())),
                           preferred_element_type=F32)


def _mm_tn(a, b):
    return lax.dot_general(a.astype(BF16), b.astype(BF16), (((0,), (0,)), ((), ())),
                           preferred_element_type=F32)


def _mm_exact_lhs(a_bf16, x):
    hi = x.astype(BF16)
    r1 = x - hi.astype(F32)
    mid = r1.astype(BF16)
    lo = (r1 - mid.astype(F32)).astype(BF16)
    d = functools.partial(jnp.dot, preferred_element_type=F32)
    return d(a_bf16, hi) + d(a_bf16, mid) + d(a_bf16, lo)


def _rms(x, w):
    ms = jnp.mean(x * x, axis=-1, keepdims=True)
    return x * lax.rsqrt(ms + NORM_EPS) * w


def _softplus(u):
    return jnp.maximum(u, 0.0) + jnp.log(1.0 + jnp.exp(-jnp.abs(u)))


def _sigmoid(u):
    return 1.0 / (1.0 + jnp.exp(-u))


def _silu(u):
    return u * _sigmoid(u)


def _shifted(z, prev_row, next_row):
    tp = z.shape[0]
    row = lax.broadcasted_iota(jnp.int32, (tp, 1), 0)
    z_dn = jnp.where(row == 0, prev_row, pltpu.roll(z, 1, 0))
    z_up = jnp.where(row == tp - 1, next_row, pltpu.roll(z, tp - 1, 0))
    return z_dn, z_up


def _halo_rows(zp_ref, zn_ref):
    t = pl.program_id(1)
    nt = pl.num_programs(1)
    prev_row = jnp.where(t > 0, zp_ref[0, 7:8, :], 0.0)
    next_row = jnp.where(t < nt - 1, zn_ref[0, 0:1, :], 0.0)
    return prev_row, next_row


def _halo_specs(tp, width, t_len):
    nb8 = t_len // 8
    per = tp // 8
    main = pl.BlockSpec((1, tp, width), lambda b, t: (b, t, 0))
    prev = pl.BlockSpec((1, 8, width), lambda b, t: (b, jnp.maximum(t * per - 1, 0), 0))
    nxt = pl.BlockSpec((1, 8, width), lambda b, t: (b, jnp.minimum((t + 1) * per, nb8 - 1), 0))
    return [main, prev, nxt]


def _pair_stack(z, head0_lanes):
    zero = jnp.zeros_like(z)
    return jnp.concatenate([jnp.where(head0_lanes, z, zero), jnp.where(head0_lanes, zero, z)], axis=0)


def _pair_masks(reverse):
    n = 2 * CHUNK
    r = lax.broadcasted_iota(jnp.int32, (n, n), 0)
    c = lax.broadcasted_iota(jnp.int32, (n, n), 1)
    same = (r // CHUNK) == (c // CHUNK)
    tt = r % CHUNK
    ss = c % CHUNK
    if reverse:
        incl = same & (ss >= tt)
        strict = same & (ss > tt)
    else:
        incl = same & (ss <= tt)
        strict = same & (ss < tt)
    return incl, strict


def _chunk_causal(reverse):
    r = lax.broadcasted_iota(jnp.int32, (CHUNK, CHUNK), 0)
    c = lax.broadcasted_iota(jnp.int32, (CHUNK, CHUNK), 1)
    return (c >= r) if reverse else (c <= r)


def _const_spec(arr, grid_rank, single_buffer=False):
    idx = (lambda i: (0,) * arr.ndim) if grid_rank == 1 else (lambda b, t: (0,) * arr.ndim)
    if single_buffer:
        return pl.BlockSpec(arr.shape, idx, pipeline_mode=pl.Buffered(1))
    return pl.BlockSpec(arr.shape, idx)


def _inproj_kernel(x_ref, nw_ref, wr_ref, wg_ref, zr_ref, zg_ref):
    xn = _rms(x_ref[...], nw_ref[...]).astype(BF16)
    zr_ref[...] = jnp.dot(xn, wr_ref[...], preferred_element_type=F32)
    zg_ref[...] = jnp.dot(xn, wg_ref[...], preferred_element_type=F32)


def _inproj(x2, nw, w_r, w_g):
    m, d = x2.shape
    tm = ROW_TILE
    nr, ng = w_r.shape[1], w_g.shape[1]
    return pl.pallas_call(
        _inproj_kernel,
        grid=(m // tm,),
        in_specs=[pl.BlockSpec((tm, d), lambda i: (i, 0)), _const_spec(nw, 1),
                  _const_spec(w_r, 1), _const_spec(w_g, 1)],
        out_specs=[pl.BlockSpec((tm, nr), lambda i: (i, 0)),
                   pl.BlockSpec((tm, ng), lambda i: (i, 0))],
        out_shape=[jax.ShapeDtypeStruct((m, nr), F32), jax.ShapeDtypeStruct((m, ng), F32)],
        compiler_params=pltpu.CompilerParams(dimension_semantics=("parallel",),
                                             vmem_limit_bytes=VMEM_LIMIT),
        name="inproj",
    )(x2, nw, w_r, w_g)


def _rwkv_prep_kernel(z_ref, zp_ref, zn_ref, mu_ref, w0_ref, wup_ref, a0_ref, aup_ref, gup_ref,
                      kk_ref, ka_ref, seg_ref, pack_ref, lw_ref, *, width):
    w = width
    z = z_ref[0]
    prev_row, next_row = _halo_rows(zp_ref, zn_ref)
    z_dn, z_up = _shifted(z, prev_row, next_row)
    zs = z + mu_ref[...] * (0.5 * (z_dn + z_up) - z)
    r = zs[:, 0:w]
    k = zs[:, w:2 * w]
    v = zs[:, 2 * w:3 * w]
    w_lo = zs[:, 3 * w:3 * w + LANES]
    a_lo = zs[:, 3 * w + LANES:3 * w + 2 * LANES]
    g_lo = zs[:, 3 * w + 2 * LANES:3 * w + 3 * LANES]
    w_log = -_softplus(-(w0_ref[...] + _mm(jnp.tanh(w_lo), wup_ref[...]))) - 0.5
    lw_ref[0] = -jnp.exp(w_log)
    a = _sigmoid(a0_ref[...] + _mm(a_lo, aup_ref[...]))
    g = _mm(_sigmoid(g_lo), gup_ref[...])
    kk = k * kk_ref[...]
    ss = _mm(kk * kk, seg_ref[...])
    kk = kk * lax.rsqrt(jnp.maximum(ss, 1e-24))
    k = k * (1.0 + (a - 1.0) * ka_ref[...])
    for j, val in enumerate((r, k, v, kk, kk * a, g)):
        pack_ref[0, :, j * w:(j + 1) * w] = val.astype(pack_ref.dtype)


def _rwkv_prep(z_r, mu, w0, w_up, a0, a_up, g_up, k_k, k_a, seg, width):
    bsz, t_len, zw = z_r.shape
    tp = ROW_TILE
    consts = (mu, w0, w_up, a0, a_up, g_up, k_k, k_a, seg)
    return pl.pallas_call(
        functools.partial(_rwkv_prep_kernel, width=width),
        grid=(bsz, t_len // tp),
        in_specs=_halo_specs(tp, zw, t_len) + [_const_spec(c, 2) for c in consts],
        out_specs=[pl.BlockSpec((1, tp, 6 * width), lambda b, t: (b, t, 0)),
                   pl.BlockSpec((1, tp, 2 * width), lambda b, t: (b, t, 0))],
        out_shape=[jax.ShapeDtypeStruct((bsz, t_len, 6 * width), BF16),
                   jax.ShapeDtypeStruct((bsz, t_len, 2 * width), F32)],
        compiler_params=pltpu.CompilerParams(dimension_semantics=("parallel", "parallel"),
                                             vmem_limit_bytes=VMEM_LIMIT),
        name="rwkv_prep",
    )(z_r, z_r, z_r, *consts)


def _neumann_inverses(mats):
    n = mats[0].shape[0]
    eye = (lax.broadcasted_iota(jnp.int32, (n, n), 0) == lax.broadcasted_iota(jnp.int32, (n, n), 1))
    ts = [a + eye.astype(F32) for a in mats]
    ps = [_mm(a, a) for a in mats]
    levels = CHUNK.bit_length() - 1
    for level in range(2, levels):
        prods = [_mm(p, jnp.concatenate([t, p], axis=1)) for t, p in zip(ts, ps)]
        ts = [t + pr[:, :n] for t, pr in zip(ts, prods)]
        ps = [pr[:, n:] for pr in prods]
    return [t + _mm(p, t) for t, p in zip(ts, ps)]


def _delta_chunk_operands(refs, reverse):
    r_ref, k_ref, v_ref, kk_ref, b_ref, lw_ref = refs
    lw = lw_ref[0]
    tri = _chunk_causal(reverse).astype(BF16)
    g = _mm_exact_lhs(tri, lw)
    total = jnp.sum(lw, axis=0, keepdims=True)
    einv = jnp.exp(-g)
    ec = jnp.exp(total - g)
    kk = kk_ref[0].astype(F32)
    bb = b_ref[0].astype(F32)
    kx = k_ref[0].astype(F32)
    full = dict(a=-kk * jnp.exp(g - lw), r=r_ref[0].astype(F32) * jnp.exp(g), b=bb * einv, k=kx * einv,
                v=v_ref[0], bh=bb * ec, kh=kx * ec)
    head0 = lax.broadcasted_iota(jnp.int32, (CHUNK, LANES), 1) < RWKV_HEAD_DIM
    incl, strict = _pair_masks(reverse)
    gam = jnp.exp(total)
    probs = []
    for p in range(lw.shape[1] // LANES):
        sl = slice(LANES * p, LANES * (p + 1))
        prob = {name: _pair_stack(z[:, sl], head0) for name, z in full.items()}
        prob.update(incl=incl, strict=strict, gam=gam[:, sl], lanes=sl)
        probs.append(prob)
    return probs


def _rwkv_scan_kernel(*refs):
    in_f, in_b = refs[0:6], refs[6:12]
    yf_ref, yb_ref, ht_ref = refs[12:15]

    @pl.when(pl.program_id(1) == 0)
    def _():
        ht_ref[...] = jnp.zeros_like(ht_ref)

    c = CHUNK
    n = 2 * c
    zero = jnp.zeros((n, n), F32)
    probs = _delta_chunk_operands(in_f, False) + _delta_chunk_operands(in_b, True)
    n_pairs = len(probs) // 2
    outs = [yf_ref] * n_pairs + [yb_ref] * n_pairs
    gmats = [_mm_nt(jnp.concatenate([q["a"], q["r"]], axis=0), jnp.concatenate([q["b"], q["k"]], axis=0))
             for q in probs]
    a_ab = [jnp.where(q["strict"], gm[:n, :n], zero) for q, gm in zip(probs, gmats)]
    a_ak = [jnp.where(q["strict"], gm[:n, n:], zero) for q, gm in zip(probs, gmats)]
    a_rb = [jnp.where(q["incl"], gm[n:, :n], zero) for q, gm in zip(probs, gmats)]
    a_rk = [jnp.where(q["incl"], gm[n:, n:], zero) for q, gm in zip(probs, gmats)]
    akv = [_mm(a, q["v"]) for a, q in zip(a_ak, probs)]
    y0 = [_mm(a, q["v"]) for a, q in zip(a_rk, probs)]
    kv = [_mm_tn(q["v"], q["kh"]) for q in probs]
    tinv = _neumann_inverses(a_ab)
    w12 = [_mm(t, jnp.concatenate([q["a"], x], axis=1)) for t, q, x in zip(tinv, probs, akv)]
    rw = [_mm(a, w) for a, w in zip(a_rb, w12)]
    m12 = [_mm_tn(w, q["bh"]) for w, q in zip(w12, probs)]
    hts = [ht_ref[i] for i in range(len(probs))]
    ys = [_mm_nt(q["r"] + x[:, :n], h) + x[:, n:] + y for q, x, h, y in zip(probs, rw, hts, y0)]
    hm = [_mm(h, m[:n]) for h, m in zip(hts, m12)]
    for i, q in enumerate(probs):
        outs[i][0, :, q["lanes"]] = ys[i][:c] + ys[i][c:]
        ht_ref[i] = hts[i] * q["gam"] + hm[i] + m12[i][n:] + kv[i]


def _rwkv_scan(pack, lw, width):
    bsz, t_len, _ = pack.shape
    nc = t_len // CHUNK
    fwd = lambda j: pl.BlockSpec((1, CHUNK, width), lambda b, t: (b, t, j))
    bwd = lambda j: pl.BlockSpec((1, CHUNK, width), lambda b, t: (b, nc - 1 - t, j))
    n_pairs = width // LANES
    y_shape = jax.ShapeDtypeStruct((bsz, t_len, width), F32)
    return pl.pallas_call(
        _rwkv_scan_kernel,
        grid=(bsz, nc),
        in_specs=[fwd(j) for j in (0, 1, 2, 3, 4)] + [fwd(0)] + [bwd(j) for j in (0, 1, 2, 3, 4)] + [bwd(1)],
        out_specs=[fwd(0), bwd(0)],
        out_shape=[y_shape, y_shape],
        scratch_shapes=[pltpu.VMEM((2 * n_pairs, 2 * CHUNK, 2 * CHUNK), F32)],
        compiler_params=pltpu.CompilerParams(dimension_semantics=("parallel", "arbitrary"),
                                             vmem_limit_bytes=VMEM_LIMIT),
        name="rwkv_scan",
    )(*([pack] * 5 + [lw] + [pack] * 5 + [lw]))


def _gla_prep_kernel(z_ref, zp_ref, zn_ref, cw_ref, aup_ref, ab_ref, pack_ref, la_ref, *, kw, vw):
    qkv_w = 2 * kw + vw
    z = z_ref[0]
    prev_row, next_row = _halo_rows(zp_ref, zn_ref)
    x = z[:, :qkv_w]
    x_dn, x_up = _shifted(x, prev_row[:, :qkv_w], next_row[:, :qkv_w])
    qkv = _silu(cw_ref[0:1, :] * x_dn + cw_ref[1:2, :] * x + cw_ref[2:3, :] * x_up)
    a_lo = z[:, qkv_w + vw:qkv_w + vw + LANES]
    logit = _mm(a_lo, aup_ref[...]) + ab_ref[...]
    la_ref[0] = -_softplus(-logit) * (1.0 / GLA_GATE_TEMP)
    dk = kw // GLA_HEADS
    pack_ref[0, :, 0:kw] = (qkv[:, 0:kw] * (dk ** -0.5)).astype(pack_ref.dtype)
    pack_ref[0, :, kw:qkv_w] = qkv[:, kw:qkv_w].astype(pack_ref.dtype)


def _gla_prep(z_g, conv_w, a_up, a_b, kw, vw):
    bsz, t_len, zw = z_g.shape
    tp = ROW_TILE
    consts = (conv_w, a_up, a_b)
    return pl.pallas_call(
        functools.partial(_gla_prep_kernel, kw=kw, vw=vw),
        grid=(bsz, t_len // tp),
        in_specs=_halo_specs(tp, zw, t_len) + [_const_spec(c, 2) for c in consts],
        out_specs=[pl.BlockSpec((1, tp, 2 * kw + vw), lambda b, t: (b, t, 0)),
                   pl.BlockSpec((1, tp, 2 * kw), lambda b, t: (b, t, 0))],
        out_shape=[jax.ShapeDtypeStruct((bsz, t_len, 2 * kw + vw), BF16),
                   jax.ShapeDtypeStruct((bsz, t_len, 2 * kw), F32)],
        compiler_params=pltpu.CompilerParams(dimension_semantics=("parallel", "parallel"),
                                             vmem_limit_bytes=VMEM_LIMIT),
        name="gla_prep",
    )(z_g, z_g, z_g, *consts)


def _gla_block_operands(refs, reverse, slot0, out_ref):
    q_ref, k_ref, v_ref, la_ref = refs
    c = CHUNK
    nb = la_ref.shape[1] // c
    tri = _chunk_causal(reverse)
    tri_bf = tri.astype(BF16)
    head0 = lax.broadcasted_iota(jnp.int32, (c, LANES), 1) < (LANES // 2)
    zero = jnp.zeros((c, LANES), F32)
    probs = []
    order = range(nb - 1, -1, -1) if reverse else range(nb)
    for step, ci in enumerate(order):
        rows = slice(ci * c, (ci + 1) * c)
        la = la_ref[0, rows, :]
        bc = _mm_exact_lhs(tri_bf, la)
        total = jnp.sum(la, axis=0, keepdims=True)
        kx = k_ref[0, rows, :].astype(F32)
        q_t = q_ref[0, rows, :].astype(F32) * jnp.exp(bc)
        k_t = kx * jnp.exp(-bc)
        k_h = kx * jnp.exp(total - bc)
        gam = jnp.exp(total)
        for p in range(la.shape[1] // LANES):
            sl = slice(LANES * p, LANES * (p + 1))
            for h in range(2):
                hm = head0 if h == 0 else jnp.logical_not(head0)
                vsl = slice(LANES * (2 * p + h), LANES * (2 * p + h + 1))
                probs.append(dict(q=jnp.where(hm, q_t[:, sl], zero), kt=k_t[:, sl],
                                  kh=jnp.where(hm, k_h[:, sl], zero), v=v_ref[0, rows, vsl], causal=tri,
                                  gam=gam[:, sl], slot=slot0 + p, step=step, rows=rows, lanes=vsl,
                                  out=out_ref))
    return probs


def _gla_scan_kernel(*refs):
    in_f, in_b = refs[0:4], refs[4:8]
    of_ref, ob_ref, st_ref = refs[8:11]

    @pl.when(pl.program_id(1) == 0)
    def _():
        st_ref[...] = jnp.zeros_like(st_ref)

    n_slots = st_ref.shape[0]
    probs = (_gla_block_operands(in_f, False, 0, of_ref)
             + _gla_block_operands(in_b, True, n_slots // 2, ob_ref))
    n_steps = 1 + max(q["step"] for q in probs)
    scores = [jnp.where(q["causal"], _mm_nt(q["q"], q["kt"]), 0.0) for q in probs]
    intra = [_mm(s, q["v"]) for s, q in zip(scores, probs)]
    upd = [_mm_tn(q["v"], q["kh"]) for q in probs]
    state = [st_ref[s] for s in range(n_slots)]
    entering = [None] * len(probs)
    for step in range(n_steps):
        for s in range(n_slots):
            mine = [i for i, q in enumerate(probs) if q["slot"] == s and q["step"] == step]
            for i in mine:
                entering[i] = state[s]
            state[s] = state[s] * probs[mine[0]]["gam"] + upd[mine[0]] + upd[mine[1]]
    inter = [_mm_nt(q["q"], st) for q, st in zip(probs, entering)]
    for i, q in enumerate(probs):
        q["out"][0, q["rows"], q["lanes"]] = intra[i] + inter[i]
    for s in range(n_slots):
        st_ref[s] = state[s]


def _gla_scan(pack, la, kw, vw):
    bsz, t_len, _ = pack.shape
    rows = GLA_BLOCK_CHUNKS * CHUNK
    nb = t_len // rows
    fwd = lambda w, j: pl.BlockSpec((1, rows, w), lambda b, t: (b, t, j))
    bwd = lambda w, j: pl.BlockSpec((1, rows, w), lambda b, t: (b, nb - 1 - t, j))
    o_shape = jax.ShapeDtypeStruct((bsz, t_len, vw), F32)
    specs = lambda blk, la_col: [blk(kw, 0), blk(kw, 1), blk(vw, (2 * kw) // vw), blk(kw, la_col)]
    return pl.pallas_call(
        _gla_scan_kernel,
        grid=(bsz, nb),
        in_specs=specs(fwd, 0) + specs(bwd, 1),
        out_specs=[fwd(vw, 0), bwd(vw, 0)],
        out_shape=[o_shape, o_shape],
        scratch_shapes=[pltpu.VMEM((2 * (kw // LANES), LANES, LANES), F32)],
        compiler_params=pltpu.CompilerParams(dimension_semantics=("parallel", "arbitrary"),
                                             vmem_limit_bytes=VMEM_LIMIT),
        name="gla_scan",
    )(*([pack] * 3 + [la] + [pack] * 3 + [la]))


def _tail_kernel(x_ref, yf_ref, yb_ref, r_ref, k_ref, v_ref, g_ref, of_ref, ob_ref, zg_ref, p_ref,
                 seg_ref, rk_ref, lnw_ref, lnb_ref, gn_ref, wor_ref, wog_ref, nmix_ref,
                 nfpre_ref, nfpost_ref, wgate_ref, wup_ref, wdown_ref, wpp_ref, wpg_ref, bpg_ref, nple_ref,
                 out_ref):
    dot = functools.partial(jnp.dot, preferred_element_type=F32)
    seg = seg_ref[...]
    inv_n = 1.0 / RWKV_HEAD_DIM
    y = yf_ref[...] + yb_ref[...]
    mu = _mm(y, seg) * inv_n
    yc = y - mu
    var = _mm(yc * yc, seg) * inv_n
    yn = yc * lax.rsqrt(var + RWKV_GN_EPS) * lnw_ref[...] + lnb_ref[...]
    r = r_ref[...].astype(F32)
    bonus = _mm(r * k_ref[...].astype(F32) * rk_ref[...], seg) * v_ref[...].astype(F32)
    y_r = ((yn + bonus) * g_ref[...].astype(F32)).astype(BF16)
    o = of_ref[...] + ob_ref[...]
    gate_g = zg_ref[...]
    y_g = jnp.concatenate(
        [_rms(o[:, LANES * h:LANES * (h + 1)], gn_ref[...]) * _silu(gate_g[:, LANES * h:LANES * (h + 1)])
         for h in range(GLA_HEADS)], axis=1).astype(BF16)
    h1 = x_ref[...] + _rms(dot(y_r, wor_ref[...]) + dot(y_g, wog_ref[...]), nmix_ref[...])
    hn = _rms(h1, nfpre_ref[...]).astype(BF16)
    act = (_silu(dot(hn, wgate_ref[...])) * dot(hn, wup_ref[...])).astype(BF16)
    h2 = h1 + _rms(dot(act, wdown_ref[...]), nfpost_ref[...])
    e = dot(p_ref[...].astype(BF16), wpp_ref[...])
    gate = _sigmoid(dot(h2.astype(BF16), wpg_ref[...]) + bpg_ref[...])
    out_ref[...] = h2 + _rms(gate * e, nple_ref[...])


def _tail(x2, y_f, y_b, pack, o_f, o_b, z_g, p2, consts, g_col):
    m, d = x2.shape
    tm = ROW_TILE
    w = y_f.shape[1]
    row = lambda width, j=0: pl.BlockSpec((tm, width), lambda i: (i, j))
    big = lambda arr: arr.size * arr.dtype.itemsize >= (1 << 20)
    return pl.pallas_call(
        _tail_kernel,
        grid=(m // tm,),
        in_specs=[row(d), row(w), row(w), row(w, 0), row(w, 1), row(w, 2), row(w, 5), row(w), row(w),
                  row(w, g_col), row(p2.shape[1])] + [_const_spec(c, 1, single_buffer=big(c)) for c in consts],
        out_specs=row(d),
        out_shape=jax.ShapeDtypeStruct((m, d), F32),
        compiler_params=pltpu.CompilerParams(dimension_semantics=("parallel",),
                                             vmem_limit_bytes=VMEM_LIMIT),
        name="tail",
    )(x2, y_f, y_b, pack, pack, pack, pack, o_f, o_b, z_g, p2, *consts)


def _pad_cols(w, n):
    return jnp.pad(w, ((0, 0), (0, n - w.shape[1])))


def _layer(h, p_i, norm_mix_pre, norm_mix_post, norm_ffn_pre, norm_ffn_post, norm_ple,
           w_in, rwkv_mu, rwkv_w0, rwkv_w_up, rwkv_a0, rwkv_a_up, rwkv_g_up,
           rwkv_k_k, rwkv_k_a, rwkv_r_k, rwkv_ln_w, rwkv_ln_b,
           gla_conv, gla_a_up, gla_a_b, gla_norm, w_out,
           ffn_gate, ffn_up, ffn_down, ple_proj, ple_gate, ple_gate_b):
    bsz, t_len, d = h.shape
    m = bsz * t_len
    rw_w = rwkv_w0.shape[1]
    kw = gla_a_b.shape[1]
    vw = w_out.shape[0] - rw_w
    assert vw == rw_w and t_len % (GLA_BLOCK_CHUNKS * CHUNK) == 0 and m % ROW_TILE == 0
    rwkv_in = 3 * rw_w + 2 * RWKV_DECAY_LORA + RWKV_AAA_LORA + RWKV_GATE_LORA
    row = lambda vec: vec.reshape(1, -1)

    lo_w = 3 * rw_w + 2 * RWKV_DECAY_LORA
    a_end = lo_w + RWKV_AAA_LORA

    def rwkv_cols(wmat):
        return jnp.concatenate([wmat[:, :lo_w], _pad_cols(wmat[:, lo_w:a_end], LANES),
                                wmat[:, a_end:rwkv_in]], axis=1)

    w_r = rwkv_cols(w_in[:, :rwkv_in]).astype(BF16)
    mu = rwkv_cols(row(rwkv_mu))
    gla_in = w_in.shape[1] - rwkv_in
    gla_main = gla_in - 2 * GLA_GATE_LORA
    w_g = jnp.concatenate([w_in[:, rwkv_in:rwkv_in + gla_main],
                           _pad_cols(w_in[:, rwkv_in + gla_main:], LANES)], axis=1).astype(BF16)

    zeros = jnp.zeros((RWKV_DECAY_LORA, rw_w), F32)
    w_up = jnp.concatenate([jnp.concatenate([rwkv_w_up[0], zeros], axis=1),
                            jnp.concatenate([zeros, rwkv_w_up[1]], axis=1)], axis=0).astype(BF16)
    a_up = jnp.pad(rwkv_a_up, ((0, LANES - RWKV_AAA_LORA), (0, 0))).astype(BF16)
    g_up = rwkv_g_up.astype(BF16)
    gz = jnp.zeros((GLA_GATE_LORA, kw), F32)
    gla_up = jnp.concatenate([jnp.concatenate([gla_a_up[0], gz], axis=1),
                              jnp.concatenate([gz, gla_a_up[1]], axis=1)], axis=0)
    gla_up = jnp.pad(gla_up, ((0, LANES - 2 * GLA_GATE_LORA), (0, 0))).astype(BF16)
    head_id = jnp.arange(rw_w) // RWKV_HEAD_DIM
    seg = (head_id[:, None] == head_id[None, :]).astype(BF16)

    x2 = h.reshape(m, d)
    z_r, z_g = _inproj(x2, row(norm_mix_pre), w_r, w_g)

    pack, lw = _rwkv_prep(z_r.reshape(bsz, t_len, -1), mu, rwkv_w0.reshape(1, -1), w_up, row(rwkv_a0),
                          a_up, g_up, row(rwkv_k_k), row(rwkv_k_a), seg, rw_w)
    y_f, y_b = _rwkv_scan(pack, lw, rw_w)

    gpack, la = _gla_prep(z_g.reshape(bsz, t_len, -1), gla_conv.reshape(gla_conv.shape[0], -1), gla_up,
                          gla_a_b.reshape(1, -1), kw, vw)
    o_f, o_b = _gla_scan(gpack, la, kw, vw)

    w_o = w_out.astype(BF16)
    consts = (seg, row(rwkv_r_k), row(rwkv_ln_w), row(rwkv_ln_b), row(gla_norm), w_o[:rw_w], w_o[rw_w:],
              row(norm_mix_post), row(norm_ffn_pre), row(norm_ffn_post), ffn_gate.astype(BF16),
              ffn_up.astype(BF16), ffn_down.astype(BF16), ple_proj.astype(BF16), ple_gate.astype(BF16),
              row(ple_gate_b), row(norm_ple))
    out = _tail(x2, y_f.reshape(m, rw_w), y_b.reshape(m, rw_w), pack.reshape(m, -1), o_f.reshape(m, vw),
                o_b.reshape(m, vw), z_g, p_i.reshape(m, -1), consts, (2 * kw + vw) // vw)
    return out.reshape(bsz, t_len, d)


def kernel(x, p, norm_mix_pre, norm_mix_post, norm_ffn_pre, norm_ffn_post, norm_ple, w_in, rwkv_mu, rwkv_w0, rwkv_w_up, rwkv_a0, rwkv_a_up, rwkv_g_up, rwkv_k_k, rwkv_k_a, rwkv_r_k, rwkv_ln_w, rwkv_ln_b, gla_conv, gla_a_up, gla_a_b, gla_norm, w_out, ffn_gate, ffn_up, ffn_down, ple_proj, ple_gate, ple_gate_b):
    params = (norm_mix_pre, norm_mix_post, norm_ffn_pre, norm_ffn_post, norm_ple, w_in, rwkv_mu,
              rwkv_w0, rwkv_w_up, rwkv_a0, rwkv_a_up, rwkv_g_up, rwkv_k_k, rwkv_k_a, rwkv_r_k,
              rwkv_ln_w, rwkv_ln_b, gla_conv, gla_a_up, gla_a_b, gla_norm, w_out, ffn_gate, ffn_up,
              ffn_down, ple_proj, ple_gate, ple_gate_b)
    h = x
    for i in range(p.shape[0]):
        h = _layer(h, p[i], *(w[i] for w in params))
    return h
```

```python
import functools

import jax
import jax.numpy as jnp
from jax import lax
from jax.experimental import pallas as pl
from jax.experimental.pallas import tpu as pltpu

F32 = jnp.float32
BF16 = jnp.bfloat16

NORM_EPS = 1e-6
RWKV_GN_EPS = 64e-5
RWKV_HEAD_DIM = 64
RWKV_DECAY_LORA = 64
RWKV_AAA_LORA = 64
RWKV_GATE_LORA = 128
GLA_HEADS = 4
GLA_GATE_LORA = 16
GLA_GATE_TEMP = 16.0
CHUNK = 64
GLA_BLOCK_CHUNKS = 4
LANES = 128
VMEM_LIMIT = 56 * 1024 * 1024
ROW_TILE = 256


def _mm(a, b):
    return jnp.dot(a.astype(BF16), b.astype(BF16), preferred_element_type=F32)


def _mm_nt(a, b):
    return lax.dot_general(a.astype(BF16), b.astype(BF16), (((1,), (1,)), ((), ())),
                           preferred_element_type=F32)


def _mm_tn(a, b):
    return lax.dot_general(a.astype(BF16), b.astype(BF16), (((0,), (0,)), ((), ())),
                           preferred_element_type=F32)


def _mm_exact_lhs(a_bf16, x):
    hi = x.astype(BF16)
    r1 = x - hi.astype(F32)
    mid = r1.astype(BF16)
    lo = (r1 - mid.astype(F32)).astype(BF16)
    d = functools.partial(jnp.dot, preferred_element_type=F32)
    return d(a_bf16, hi) + d(a_bf16, mid) + d(a_bf16, lo)


def _mm_split_lhs(a_bf16, x):
    hi = x.astype(BF16)
    lo = (x - hi.astype(F32)).astype(BF16)
    d = functools.partial(jnp.dot, preferred_element_type=F32)
    return d(a_bf16, hi) + d(a_bf16, lo)


def _rms(x, w):
    ms = jnp.mean(x * x, axis=-1, keepdims=True)
    return x * lax.rsqrt(ms + NORM_EPS) * w


def _softplus(u):
    return jnp.maximum(u, 0.0) + jnp.log(1.0 + jnp.exp(-jnp.abs(u)))


def _sigmoid(u):
    return 1.0 / (1.0 + jnp.exp(-u))


def _silu(u):
    return u * _sigmoid(u)


def _shifted(z, prev_row, next_row):
    tp = z.shape[0]
    row = lax.broadcasted_iota(jnp.int32, (tp, 1), 0)
    z_dn = jnp.where(row == 0, prev_row, pltpu.roll(z, 1, 0))
    z_up = jnp.where(row == tp - 1, next_row, pltpu.roll(z, tp - 1, 0))
    return z_dn, z_up


def _halo_rows(zp_ref, zn_ref):
    t = pl.program_id(1)
    nt = pl.num_programs(1)
    prev_row = jnp.where(t > 0, zp_ref[0, 7:8, :], 0.0)
    next_row = jnp.where(t < nt - 1, zn_ref[0, 0:1, :], 0.0)
    return prev_row, next_row


def _halo_specs(tp, width, t_len):
    nb8 = t_len // 8
    per = tp // 8
    main = pl.BlockSpec((1, tp, width), lambda b, t: (b, t, 0))
    prev = pl.BlockSpec((1, 8, width), lambda b, t: (b, jnp.maximum(t * per - 1, 0), 0))
    nxt = pl.BlockSpec((1, 8, width), lambda b, t: (b, jnp.minimum((t + 1) * per, nb8 - 1), 0))
    return [main, prev, nxt]


def _pair_stack(z, head0_lanes):
    zero = jnp.zeros_like(z)
    return jnp.concatenate([jnp.where(head0_lanes, z, zero), jnp.where(head0_lanes, zero, z)], axis=0)


def _pair_masks(reverse):
    n = 2 * CHUNK
    r = lax.broadcasted_iota(jnp.int32, (n, n), 0)
    c = lax.broadcasted_iota(jnp.int32, (n, n), 1)
    same = (r // CHUNK) == (c // CHUNK)
    tt = r % CHUNK
    ss = c % CHUNK
    if reverse:
        incl = same & (ss >= tt)
        strict = same & (ss > tt)
    else:
        incl = same & (ss <= tt)
        strict = same & (ss < tt)
    return incl, strict


def _chunk_causal(reverse):
    r = lax.broadcasted_iota(jnp.int32, (CHUNK, CHUNK), 0)
    c = lax.broadcasted_iota(jnp.int32, (CHUNK, CHUNK), 1)
    return (c >= r) if reverse else (c <= r)


def _const_spec(arr, grid_rank, single_buffer=False):
    idx = (lambda i: (0,) * arr.ndim) if grid_rank == 1 else (lambda b, t: (0,) * arr.ndim)
    if single_buffer:
        return pl.BlockSpec(arr.shape, idx, pipeline_mode=pl.Buffered(1))
    return pl.BlockSpec(arr.shape, idx)


def _inproj_kernel(x_ref, nw_ref, wr_ref, wg_ref, zr_ref, zg_ref):
    xn = _rms(x_ref[...], nw_ref[...]).astype(BF16)
    zr_ref[...] = jnp.dot(xn, wr_ref[...], preferred_element_type=F32)
    zg_ref[...] = jnp.dot(xn, wg_ref[...], preferred_element_type=F32)


def _inproj(x2, nw, w_r, w_g):
    m, d = x2.shape
    tm = ROW_TILE
    nr, ng = w_r.shape[1], w_g.shape[1]
    return pl.pallas_call(
        _inproj_kernel,
        grid=(m // tm,),
        in_specs=[pl.BlockSpec((tm, d), lambda i: (i, 0)), _const_spec(nw, 1),
                  _const_spec(w_r, 1), _const_spec(w_g, 1)],
        out_specs=[pl.BlockSpec((tm, nr), lambda i: (i, 0)),
                   pl.BlockSpec((tm, ng), lambda i: (i, 0))],
        out_shape=[jax.ShapeDtypeStruct((m, nr), F32), jax.ShapeDtypeStruct((m, ng), F32)],
        compiler_params=pltpu.CompilerParams(dimension_semantics=("parallel",),
                                             vmem_limit_bytes=VMEM_LIMIT),
        name="inproj",
    )(x2, nw, w_r, w_g)


def _rwkv_prep_kernel(z_ref, zp_ref, zn_ref, mu_ref, w0_ref, wup_ref, a0_ref, aup_ref, gup_ref,
                      kk_ref, ka_ref, seg_ref, trif_ref, trib_ref, pack_ref, g_ref, *, width):
    w = width
    z = z_ref[0]
    prev_row, next_row = _halo_rows(zp_ref, zn_ref)
    z_dn, z_up = _shifted(z, prev_row, next_row)
    zs = z + mu_ref[...] * (0.5 * (z_dn + z_up) - z)
    r = zs[:, 0:w]
    k = zs[:, w:2 * w]
    v = zs[:, 2 * w:3 * w]
    w_lo = zs[:, 3 * w:3 * w + LANES]
    a_lo = zs[:, 3 * w + LANES:3 * w + 2 * LANES]
    g_lo = zs[:, 3 * w + 2 * LANES:3 * w + 3 * LANES]
    w_log = -_softplus(-(w0_ref[...] + _mm(jnp.tanh(w_lo), wup_ref[...]))) - 0.5
    lw = -jnp.exp(w_log)
    g_ref[0, :, 0:w] = _mm_split_lhs(trif_ref[...], lw[:, 0:w])
    g_ref[0, :, w:2 * w] = _mm_split_lhs(trib_ref[...], lw[:, w:2 * w])
    a = _sigmoid(a0_ref[...] + _mm(a_lo, aup_ref[...]))
    g = _mm(_sigmoid(g_lo), gup_ref[...])
    kk = k * kk_ref[...]
    ss = _mm(kk * kk, seg_ref[...])
    kk = kk * lax.rsqrt(jnp.maximum(ss, 1e-24))
    k = k * (1.0 + (a - 1.0) * ka_ref[...])
    for j, val in enumerate((r, k, v, kk, kk * a, g)):
        pack_ref[0, :, j * w:(j + 1) * w] = val.astype(pack_ref.dtype)


def _rwkv_prep(z_r, mu, w0, w_up, a0, a_up, g_up, k_k, k_a, seg, width):
    bsz, t_len, zw = z_r.shape
    tp = ROW_TILE
    chunk_id = jnp.arange(tp) // CHUNK
    pos = jnp.arange(tp)
    same = chunk_id[:, None] == chunk_id[None, :]
    tri_f = (same & (pos[None, :] <= pos[:, None])).astype(BF16)
    tri_b = (same & (pos[None, :] >= pos[:, None])).astype(BF16)
    consts = (mu, w0, w_up, a0, a_up, g_up, k_k, k_a, seg, tri_f, tri_b)
    return pl.pallas_call(
        functools.partial(_rwkv_prep_kernel, width=width),
        grid=(bsz, t_len // tp),
        in_specs=_halo_specs(tp, zw, t_len) + [_const_spec(c, 2) for c in consts],
        out_specs=[pl.BlockSpec((1, tp, 6 * width), lambda b, t: (b, t, 0)),
                   pl.BlockSpec((1, tp, 2 * width), lambda b, t: (b, t, 0))],
        out_shape=[jax.ShapeDtypeStruct((bsz, t_len, 6 * width), BF16),
                   jax.ShapeDtypeStruct((bsz, t_len, 2 * width), F32)],
        compiler_params=pltpu.CompilerParams(dimension_semantics=("parallel", "parallel"),
                                             vmem_limit_bytes=VMEM_LIMIT),
        name="rwkv_prep",
    )(z_r, z_r, z_r, *consts)


def _delta_chunk_problems(refs, row, reverse, out_ref, slot0):
    r_ref, k_ref, v_ref, kk_ref, b_ref, g_ref = refs
    c = CHUNK
    g = g_ref[row]
    trow = lax.broadcasted_iota(jnp.int32, (c, 1), 0)
    if reverse:
        total = g[0:1, :]
        g_excl = jnp.where(trow == c - 1, 0.0, pltpu.roll(g, c - 1, 0))
    else:
        total = g[c - 1:c, :]
        g_excl = jnp.where(trow == 0, 0.0, pltpu.roll(g, 1, 0))
    einv = jnp.exp(-g)
    ec = jnp.exp(total - g)
    kk = kk_ref[row].astype(F32)
    bb = b_ref[row].astype(F32)
    kx = k_ref[row].astype(F32)
    full = dict(a=(-kk * jnp.exp(g_excl)).astype(BF16), r=(r_ref[row].astype(F32) * jnp.exp(g)).astype(BF16),
                bt=(bb * einv).astype(BF16), kt=(kx * einv).astype(BF16), v=v_ref[row],
                bh=(bb * ec).astype(BF16), kh=(kx * ec).astype(BF16))
    head0 = lax.broadcasted_iota(jnp.int32, (c, LANES), 1) < RWKV_HEAD_DIM
    tt = lax.broadcasted_iota(jnp.int32, (c, LANES), 0)
    ss = lax.broadcasted_iota(jnp.int32, (c, LANES), 1) % c
    incl = (ss >= tt) if reverse else (ss <= tt)
    strict = (ss > tt) if reverse else (ss < tt)
    probs = []
    for p in range(g.shape[1] // LANES):
        sl = slice(LANES * p, LANES * (p + 1))
        q = {name: z[:, sl] for name, z in full.items()}
        gam_col = jnp.exp(jnp.transpose(jnp.broadcast_to(total[:, sl], (LANES, LANES))))
        q.update(head0=head0, incl=incl, strict=strict, gam_col=gam_col, lanes=sl, row=row, out=out_ref,
                 slot=slot0 + p)
        probs.append(q)
    return probs


def _rwkv_scan_kernel(*refs, rows):
    in_f, in_b = refs[0:6], refs[6:12]
    yf_ref, yb_ref, h_ref = refs[12:15]

    @pl.when(pl.program_id(1) == 0)
    def _():
        h_ref[...] = jnp.zeros_like(h_ref)

    c = CHUNK
    n = LANES
    n_pairs = yf_ref.shape[-1] // LANES
    probs = []
    for row in range(rows):
        probs += _delta_chunk_problems(in_f, row, False, yf_ref, (2 * row) * n_pairs)
        probs += _delta_chunk_problems(in_b, row, True, yb_ref, (2 * row + 1) * n_pairs)
    stack = lambda z, q: _pair_stack(z.astype(BF16), q["head0"])
    rr = lax.broadcasted_iota(jnp.int32, (n, n), 0) // c
    cc = lax.broadcasted_iota(jnp.int32, (n, n), 1) // c
    same_head = rr == cc
    zero = jnp.zeros((c, n), F32)
    ar = [jnp.concatenate([q["a"], q["r"]], axis=0) for q in probs]
    gm = [_mm_nt(x, jnp.concatenate([stack(q["bt"], q), stack(q["kt"], q)], axis=0)) for x, q in zip(ar, probs)]
    hs = [h_ref[q["slot"]] for q in probs]
    xh = [_mm(x, h) for x, h in zip(ar, hs)]
    a_ab = [jnp.where(q["strict"], m[:c, :n], zero) for q, m in zip(probs, gm)]
    a_ak = [jnp.where(q["strict"], m[:c, n:], zero) for q, m in zip(probs, gm)]
    a_rbk = [jnp.where(jnp.concatenate([q["incl"], q["incl"]], axis=1), m[c:], 0.0) for q, m in zip(probs, gm)]
    vs = [stack(q["v"], q) for q in probs]
    zs = [x[:c] + _mm(a, v) for x, a, v in zip(xh, a_ak, vs)]
    ps = a_ab
    levels = CHUNK.bit_length() - 1
    for level in range(levels - 1):
        prods = [_mm(p, jnp.concatenate([stack(z, q), stack(p, q)], axis=1)) for p, z, q in zip(ps, zs, probs)]
        zs = [z + pr[:, :n] for z, pr in zip(zs, prods)]
        ps = [pr[:, n:] for pr in prods]
    us = [z + _mm(p, stack(z, q)) for p, z, q in zip(ps, zs, probs)]
    ys = [x[c:] + _mm(a, jnp.concatenate([stack(u, q), v], axis=0))
          for x, a, u, v, q in zip(xh, a_rbk, us, vs, probs)]
    upd = [_mm_tn(jnp.concatenate([q["bh"], q["kh"]], axis=0), jnp.concatenate([u.astype(BF16), q["v"]], axis=0))
           for q, u in zip(probs, us)]
    for q, y, h, d in zip(probs, ys, hs, upd):
        q["out"][q["row"], :, q["lanes"]] = y
        h_ref[q["slot"]] = h * q["gam_col"] + jnp.where(same_head, d, 0.0)


def _rwkv_scan(pack, g, width, rows):
    bsz, t_len, _ = pack.shape
    nc = t_len // CHUNK
    fwd = lambda j: pl.BlockSpec((rows, CHUNK, width), lambda b, t: (b, t, j))
    bwd = lambda j: pl.BlockSpec((rows, CHUNK, width), lambda b, t: (b, nc - 1 - t, j))
    n_pairs = width // LANES
    y_shape = jax.ShapeDtypeStruct((bsz, t_len, width), F32)
    return pl.pallas_call(
        functools.partial(_rwkv_scan_kernel, rows=rows),
        grid=(bsz // rows, nc),
        in_specs=[fwd(j) for j in (0, 1, 2, 3, 4)] + [fwd(0)] + [bwd(j) for j in (0, 1, 2, 3, 4)] + [bwd(1)],
        out_specs=[fwd(0), bwd(0)],
        out_shape=[y_shape, y_shape],
        scratch_shapes=[pltpu.VMEM((2 * rows * n_pairs, LANES, LANES), F32)],
        compiler_params=pltpu.CompilerParams(dimension_semantics=("parallel", "arbitrary"),
                                             vmem_limit_bytes=VMEM_LIMIT),
        name="rwkv_scan",
    )(*([pack] * 5 + [g] + [pack] * 5 + [g]))


def _gla_prep_kernel(z_ref, zp_ref, zn_ref, cw_ref, aup_ref, ab_ref, pack_ref, la_ref, *, kw, vw):
    qkv_w = 2 * kw + vw
    z = z_ref[0]
    prev_row, next_row = _halo_rows(zp_ref, zn_ref)
    x = z[:, :qkv_w]
    x_dn, x_up = _shifted(x, prev_row[:, :qkv_w], next_row[:, :qkv_w])
    qkv = _silu(cw_ref[0:1, :] * x_dn + cw_ref[1:2, :] * x + cw_ref[2:3, :] * x_up)
    a_lo = z[:, qkv_w + vw:qkv_w + vw + LANES]
    logit = _mm(a_lo, aup_ref[...]) + ab_ref[...]
    la_ref[0] = -_softplus(-logit) * (1.0 / GLA_GATE_TEMP)
    dk = kw // GLA_HEADS
    pack_ref[0, :, 0:kw] = (qkv[:, 0:kw] * (dk ** -0.5)).astype(pack_ref.dtype)
    pack_ref[0, :, kw:qkv_w] = qkv[:, kw:qkv_w].astype(pack_ref.dtype)


def _gla_prep(z_g, conv_w, a_up, a_b, kw, vw):
    bsz, t_len, zw = z_g.shape
    tp = ROW_TILE
    consts = (conv_w, a_up, a_b)
    return pl.pallas_call(
        functools.partial(_gla_prep_kernel, kw=kw, vw=vw),
        grid=(bsz, t_len // tp),
        in_specs=_halo_specs(tp, zw, t_len) + [_const_spec(c, 2) for c in consts],
        out_specs=[pl.BlockSpec((1, tp, 2 * kw + vw), lambda b, t: (b, t, 0)),
                   pl.BlockSpec((1, tp, 2 * kw), lambda b, t: (b, t, 0))],
        out_shape=[jax.ShapeDtypeStruct((bsz, t_len, 2 * kw + vw), BF16),
                   jax.ShapeDtypeStruct((bsz, t_len, 2 * kw), F32)],
        compiler_params=pltpu.CompilerParams(dimension_semantics=("parallel", "parallel"),
                                             vmem_limit_bytes=VMEM_LIMIT),
        name="gla_prep",
    )(z_g, z_g, z_g, *consts)


def _gla_block_operands(refs, reverse, slot0, out_ref):
    q_ref, k_ref, v_ref, la_ref = refs
    c = CHUNK
    nb = la_ref.shape[1] // c
    tri = _chunk_causal(reverse)
    tri_bf = tri.astype(BF16)
    head0 = lax.broadcasted_iota(jnp.int32, (c, LANES), 1) < (LANES // 2)
    zero = jnp.zeros((c, LANES), F32)
    probs = []
    order = range(nb - 1, -1, -1) if reverse else range(nb)
    for step, ci in enumerate(order):
        rows = slice(ci * c, (ci + 1) * c)
        la = la_ref[0, rows, :]
        bc = _mm_exact_lhs(tri_bf, la)
        total = jnp.sum(la, axis=0, keepdims=True)
        kx = k_ref[0, rows, :].astype(F32)
        q_t = q_ref[0, rows, :].astype(F32) * jnp.exp(bc)
        k_t = kx * jnp.exp(-bc)
        k_h = kx * jnp.exp(total - bc)
        gam = jnp.exp(total)
        for p in range(la.shape[1] // LANES):
            sl = slice(LANES * p, LANES * (p + 1))
            for h in range(2):
                hm = head0 if h == 0 else jnp.logical_not(head0)
                vsl = slice(LANES * (2 * p + h), LANES * (2 * p + h + 1))
                probs.append(dict(q=jnp.where(hm, q_t[:, sl], zero), kt=k_t[:, sl],
                                  kh=jnp.where(hm, k_h[:, sl], zero), v=v_ref[0, rows, vsl], causal=tri,
                                  gam=gam[:, sl], slot=slot0 + p, step=step, rows=rows, lanes=vsl,
                                  out=out_ref))
    return probs


def _gla_scan_kernel(*refs):
    in_f, in_b = refs[0:4], refs[4:8]
    of_ref, ob_ref, st_ref = refs[8:11]

    @pl.when(pl.program_id(1) == 0)
    def _():
        st_ref[...] = jnp.zeros_like(st_ref)

    n_slots = st_ref.shape[0]
    probs = (_gla_block_operands(in_f, False, 0, of_ref)
             + _gla_block_operands(in_b, True, n_slots // 2, ob_ref))
    n_steps = 1 + max(q["step"] for q in probs)
    scores = [jnp.where(q["causal"], _mm_nt(q["q"], q["kt"]), 0.0) for q in probs]
    intra = [_mm(s, q["v"]) for s, q in zip(scores, probs)]
    upd = [_mm_tn(q["v"], q["kh"]) for q in probs]
    state = [st_ref[s] for s in range(n_slots)]
    entering = [None] * len(probs)
    for step in range(n_steps):
        for s in range(n_slots):
            mine = [i for i, q in enumerate(probs) if q["slot"] == s and q["step"] == step]
            for i in mine:
                entering[i] = state[s]
            state[s] = state[s] * probs[mine[0]]["gam"] + upd[mine[0]] + upd[mine[1]]
    inter = [_mm_nt(q["q"], st) for q, st in zip(probs, entering)]
    for i, q in enumerate(probs):
        q["out"][0, q["rows"], q["lanes"]] = intra[i] + inter[i]
    for s in range(n_slots):
        st_ref[s] = state[s]


def _gla_scan(pack, la, kw, vw):
    bsz, t_len, _ = pack.shape
    rows = GLA_BLOCK_CHUNKS * CHUNK
    nb = t_len // rows
    fwd = lambda w, j: pl.BlockSpec((1, rows, w), lambda b, t: (b, t, j))
    bwd = lambda w, j: pl.BlockSpec((1, rows, w), lambda b, t: (b, nb - 1 - t, j))
    o_shape = jax.ShapeDtypeStruct((bsz, t_len, vw), F32)
    specs = lambda blk, la_col: [blk(kw, 0), blk(kw, 1), blk(vw, (2 * kw) // vw), blk(kw, la_col)]
    return pl.pallas_call(
        _gla_scan_kernel,
        grid=(bsz, nb),
        in_specs=specs(fwd, 0) + specs(bwd, 1),
        out_specs=[fwd(vw, 0), bwd(vw, 0)],
        out_shape=[o_shape, o_shape],
        scratch_shapes=[pltpu.VMEM((2 * (kw // LANES), LANES, LANES), F32)],
        compiler_params=pltpu.CompilerParams(dimension_semantics=("parallel", "arbitrary"),
                                             vmem_limit_bytes=VMEM_LIMIT),
        name="gla_scan",
    )(*([pack] * 3 + [la] + [pack] * 3 + [la]))


def _tail_kernel(x_ref, yf_ref, yb_ref, r_ref, k_ref, v_ref, g_ref, of_ref, ob_ref, zg_ref, p_ref,
                 seg_ref, rk_ref, lnw_ref, lnb_ref, gn_ref, wor_ref, wog_ref, nmix_ref,
                 nfpre_ref, nfpost_ref, wgate_ref, wup_ref, wdown_ref, wpp_ref, wpg_ref, bpg_ref, nple_ref,
                 out_ref):
    dot = functools.partial(jnp.dot, preferred_element_type=F32)
    seg = seg_ref[...]
    inv_n = 1.0 / RWKV_HEAD_DIM
    y = yf_ref[...] + yb_ref[...]
    mu = _mm(y, seg) * inv_n
    yc = y - mu
    var = _mm(yc * yc, seg) * inv_n
    yn = yc * lax.rsqrt(var + RWKV_GN_EPS) * lnw_ref[...] + lnb_ref[...]
    r = r_ref[...].astype(F32)
    bonus = _mm(r * k_ref[...].astype(F32) * rk_ref[...], seg) * v_ref[...].astype(F32)
    y_r = ((yn + bonus) * g_ref[...].astype(F32)).astype(BF16)
    o = of_ref[...] + ob_ref[...]
    gate_g = zg_ref[...]
    y_g = jnp.concatenate(
        [_rms(o[:, LANES * h:LANES * (h + 1)], gn_ref[...]) * _silu(gate_g[:, LANES * h:LANES * (h + 1)])
         for h in range(GLA_HEADS)], axis=1).astype(BF16)
    h1 = x_ref[...] + _rms(dot(y_r, wor_ref[...]) + dot(y_g, wog_ref[...]), nmix_ref[...])
    hn = _rms(h1, nfpre_ref[...]).astype(BF16)
    act = (_silu(dot(hn, wgate_ref[...])) * dot(hn, wup_ref[...])).astype(BF16)
    h2 = h1 + _rms(dot(act, wdown_ref[...]), nfpost_ref[...])
    e = dot(p_ref[...].astype(BF16), wpp_ref[...])
    gate = _sigmoid(dot(h2.astype(BF16), wpg_ref[...]) + bpg_ref[...])
    out_ref[...] = h2 + _rms(gate * e, nple_ref[...])


def _tail(x2, y_f, y_b, pack, o_f, o_b, z_g, p2, consts, g_col):
    m, d = x2.shape
    tm = ROW_TILE
    w = y_f.shape[1]
    row = lambda width, j=0: pl.BlockSpec((tm, width), lambda i: (i, j))
    big = lambda arr: arr.size * arr.dtype.itemsize >= (1 << 20)
    return pl.pallas_call(
        _tail_kernel,
        grid=(m // tm,),
        in_specs=[row(d), row(w), row(w), row(w, 0), row(w, 1), row(w, 2), row(w, 5), row(w), row(w),
                  row(w, g_col), row(p2.shape[1])] + [_const_spec(c, 1, single_buffer=big(c)) for c in consts],
        out_specs=row(d),
        out_shape=jax.ShapeDtypeStruct((m, d), F32),
        compiler_params=pltpu.CompilerParams(dimension_semantics=("parallel",),
                                             vmem_limit_bytes=VMEM_LIMIT),
        name="tail",
    )(x2, y_f, y_b, pack, pack, pack, pack, o_f, o_b, z_g, p2, *consts)


def _pad_cols(w, n):
    return jnp.pad(w, ((0, 0), (0, n - w.shape[1])))


def _layer(h, p_i, norm_mix_pre, norm_mix_post, norm_ffn_pre, norm_ffn_post, norm_ple,
           w_in, rwkv_mu, rwkv_w0, rwkv_w_up, rwkv_a0, rwkv_a_up, rwkv_g_up,
           rwkv_k_k, rwkv_k_a, rwkv_r_k, rwkv_ln_w, rwkv_ln_b,
           gla_conv, gla_a_up, gla_a_b, gla_norm, w_out,
           ffn_gate, ffn_up, ffn_down, ple_proj, ple_gate, ple_gate_b):
    bsz, t_len, d = h.shape
    m = bsz * t_len
    rw_w = rwkv_w0.shape[1]
    kw = gla_a_b.shape[1]
    vw = w_out.shape[0] - rw_w
    assert vw == rw_w and t_len % (GLA_BLOCK_CHUNKS * CHUNK) == 0 and m % ROW_TILE == 0
    rwkv_in = 3 * rw_w + 2 * RWKV_DECAY_LORA + RWKV_AAA_LORA + RWKV_GATE_LORA
    row = lambda vec: vec.reshape(1, -1)

    lo_w = 3 * rw_w + 2 * RWKV_DECAY_LORA
    a_end = lo_w + RWKV_AAA_LORA

    def rwkv_cols(wmat):
        return jnp.concatenate([wmat[:, :lo_w], _pad_cols(wmat[:, lo_w:a_end], LANES),
                                wmat[:, a_end:rwkv_in]], axis=1)

    w_r = rwkv_cols(w_in[:, :rwkv_in]).astype(BF16)
    mu = rwkv_cols(row(rwkv_mu))
    gla_in = w_in.shape[1] - rwkv_in
    gla_main = gla_in - 2 * GLA_GATE_LORA
    w_g = jnp.concatenate([w_in[:, rwkv_in:rwkv_in + gla_main],
                           _pad_cols(w_in[:, rwkv_in + gla_main:], LANES)], axis=1).astype(BF16)

    zeros = jnp.zeros((RWKV_DECAY_LORA, rw_w), F32)
    w_up = jnp.concatenate([jnp.concatenate([rwkv_w_up[0], zeros], axis=1),
                            jnp.concatenate([zeros, rwkv_w_up[1]], axis=1)], axis=0).astype(BF16)
    a_up = jnp.pad(rwkv_a_up, ((0, LANES - RWKV_AAA_LORA), (0, 0))).astype(BF16)
    g_up = rwkv_g_up.astype(BF16)
    gz = jnp.zeros((GLA_GATE_LORA, kw), F32)
    gla_up = jnp.concatenate([jnp.concatenate([gla_a_up[0], gz], axis=1),
                              jnp.concatenate([gz, gla_a_up[1]], axis=1)], axis=0)
    gla_up = jnp.pad(gla_up, ((0, LANES - 2 * GLA_GATE_LORA), (0, 0))).astype(BF16)
    head_id = jnp.arange(rw_w) // RWKV_HEAD_DIM
    seg = (head_id[:, None] == head_id[None, :]).astype(BF16)

    x2 = h.reshape(m, d)
    z_r, z_g = _inproj(x2, row(norm_mix_pre), w_r, w_g)

    pack, lw = _rwkv_prep(z_r.reshape(bsz, t_len, -1), mu, rwkv_w0.reshape(1, -1), w_up, row(rwkv_a0),
                          a_up, g_up, row(rwkv_k_k), row(rwkv_k_a), seg, rw_w)
    y_f, y_b = _rwkv_scan(pack, lw, rw_w, 2 if bsz % 2 == 0 else 1)

    gpack, la = _gla_prep(z_g.reshape(bsz, t_len, -1), gla_conv.reshape(gla_conv.shape[0], -1), gla_up,
                          gla_a_b.reshape(1, -1), kw, vw)
    o_f, o_b = _gla_scan(gpack, la, kw, vw)

    w_o = w_out.astype(BF16)
    consts = (seg, row(rwkv_r_k), row(rwkv_ln_w), row(rwkv_ln_b), row(gla_norm), w_o[:rw_w], w_o[rw_w:],
              row(norm_mix_post), row(norm_ffn_pre), row(norm_ffn_post), ffn_gate.astype(BF16),
              ffn_up.astype(BF16), ffn_down.astype(BF16), ple_proj.astype(BF16), ple_gate.astype(BF16),
              row(ple_gate_b), row(norm_ple))
    out = _tail(x2, y_f.reshape(m, rw_w), y_b.reshape(m, rw_w), pack.reshape(m, -1), o_f.reshape(m, vw),
                o_b.reshape(m, vw), z_g, p_i.reshape(m, -1), consts, (2 * kw + vw) // vw)
    return out.reshape(bsz, t_len, d)


def kernel(x, p, norm_mix_pre, norm_mix_post, norm_ffn_pre, norm_ffn_post, norm_ple, w_in, rwkv_mu, rwkv_w0, rwkv_w_up, rwkv_a0, rwkv_a_up, rwkv_g_up, rwkv_k_k, rwkv_k_a, rwkv_r_k, rwkv_ln_w, rwkv_ln_b, gla_conv, gla_a_up, gla_a_b, gla_norm, w_out, ffn_gate, ffn_up, ffn_down, ple_proj, ple_gate, ple_gate_b):
    params = (norm_mix_pre, norm_mix_post, norm_ffn_pre, norm_ffn_post, norm_ple, w_in, rwkv_mu,
              rwkv_w0, rwkv_w_up, rwkv_a0, rwkv_a_up, rwkv_g_up, rwkv_k_k, rwkv_k_a, rwkv_r_k,
              rwkv_ln_w, rwkv_ln_b, gla_conv, gla_a_up, gla_a_b, gla_norm, w_out, ffn_gate, ffn_up,
              ffn_down, ple_proj, ple_gate, ple_gate_b)
    h = x
    for i in range(p.shape[0]):
        h = _layer(h, p[i], *(w[i] for w in params))
    return h
```

```python
import functools

import jax
import jax.numpy as jnp
from jax import lax
from jax.experimental import pallas as pl
from jax.experimental.pallas import tpu as pltpu

F32 = jnp.float32
BF16 = jnp.bfloat16

NORM_EPS = 1e-6
RWKV_GN_EPS = 64e-5
RWKV_HEAD_DIM = 64
RWKV_DECAY_LORA = 64
RWKV_AAA_LORA = 64
RWKV_GATE_LORA = 128
GLA_HEADS = 4
GLA_GATE_LORA = 16
GLA_GATE_TEMP = 16.0
CHUNK = 64
GLA_BLOCK_CHUNKS = 4
LANES = 128
VMEM_LIMIT = 56 * 1024 * 1024
ROW_TILE = 256


def _mm(a, b):
    return jnp.dot(a.astype(BF16), b.astype(BF16), preferred_element_type=F32)


def _mm_nt(a, b):
    return lax.dot_general(a.astype(BF16), b.astype(BF16), (((1,), (1,)), ((), ())),
                           preferred_element_type=F32)


def _mm_tn(a, b):
    return lax.dot_general(a.astype(BF16), b.astype(BF16), (((0,), (0,)), ((), ())),
                           preferred_element_type=F32)


def _mm_exact_lhs(a_bf16, x):
    hi = x.astype(BF16)
    r1 = x - hi.astype(F32)
    mid = r1.astype(BF16)
    lo = (r1 - mid.astype(F32)).astype(BF16)
    d = functools.partial(jnp.dot, preferred_element_type=F32)
    return d(a_bf16, hi) + d(a_bf16, mid) + d(a_bf16, lo)


def _mm_split_lhs(a_bf16, x):
    hi = x.astype(BF16)
    lo = (x - hi.astype(F32)).astype(BF16)
    d = functools.partial(jnp.dot, preferred_element_type=F32)
    return d(a_bf16, hi) + d(a_bf16, lo)


def _rms(x, w):
    ms = jnp.mean(x * x, axis=-1, keepdims=True)
    return x * lax.rsqrt(ms + NORM_EPS) * w


def _softplus(u):
    return jnp.maximum(u, 0.0) + jnp.log(1.0 + jnp.exp(-jnp.abs(u)))


def _sigmoid(u):
    return 1.0 / (1.0 + jnp.exp(-u))


def _silu(u):
    return u * _sigmoid(u)


def _shifted(z, prev_row, next_row):
    tp = z.shape[0]
    row = lax.broadcasted_iota(jnp.int32, (tp, 1), 0)
    z_dn = jnp.where(row == 0, prev_row, pltpu.roll(z, 1, 0))
    z_up = jnp.where(row == tp - 1, next_row, pltpu.roll(z, tp - 1, 0))
    return z_dn, z_up


def _halo_rows(zp_ref, zn_ref):
    t = pl.program_id(1)
    nt = pl.num_programs(1)
    prev_row = jnp.where(t > 0, zp_ref[0, 7:8, :], 0.0)
    next_row = jnp.where(t < nt - 1, zn_ref[0, 0:1, :], 0.0)
    return prev_row, next_row


def _halo_specs(tp, width, t_len):
    nb8 = t_len // 8
    per = tp // 8
    main = pl.BlockSpec((1, tp, width), lambda b, t: (b, t, 0))
    prev = pl.BlockSpec((1, 8, width), lambda b, t: (b, jnp.maximum(t * per - 1, 0), 0))
    nxt = pl.BlockSpec((1, 8, width), lambda b, t: (b, jnp.minimum((t + 1) * per, nb8 - 1), 0))
    return [main, prev, nxt]


def _pair_stack(z, head0_lanes):
    zero = jnp.zeros_like(z)
    return jnp.concatenate([jnp.where(head0_lanes, z, zero), jnp.where(head0_lanes, zero, z)], axis=0)


def _pair_masks(reverse):
    n = 2 * CHUNK
    r = lax.broadcasted_iota(jnp.int32, (n, n), 0)
    c = lax.broadcasted_iota(jnp.int32, (n, n), 1)
    same = (r // CHUNK) == (c // CHUNK)
    tt = r % CHUNK
    ss = c % CHUNK
    if reverse:
        incl = same & (ss >= tt)
        strict = same & (ss > tt)
    else:
        incl = same & (ss <= tt)
        strict = same & (ss < tt)
    return incl, strict


def _chunk_causal(reverse):
    r = lax.broadcasted_iota(jnp.int32, (CHUNK, CHUNK), 0)
    c = lax.broadcasted_iota(jnp.int32, (CHUNK, CHUNK), 1)
    return (c >= r) if reverse else (c <= r)


def _const_spec(arr, grid_rank, single_buffer=False):
    idx = (lambda i: (0,) * arr.ndim) if grid_rank == 1 else (lambda b, t: (0,) * arr.ndim)
    if single_buffer:
        return pl.BlockSpec(arr.shape, idx, pipeline_mode=pl.Buffered(1))
    return pl.BlockSpec(arr.shape, idx)


HALO = 8


def _front_kernel(x_ref, xp_ref, xn_ref, nw_ref, wr_ref, wg_ref,
                  mu_ref, w0_ref, wup_ref, a0_ref, aup_ref, gup_ref, kk_ref, ka_ref, seg_ref, trif_ref, trib_ref,
                  cw_ref, gaup_ref, gab_ref,
                  pack_ref, g_ref, gpack_ref, la_ref, ggate_ref, *, width, kw, vw):
    t = pl.program_id(1)
    nt = pl.num_programs(1)
    tp = x_ref.shape[1]
    rows = tp + 2 * HALO
    xa = jnp.concatenate([xp_ref[0], x_ref[0], xn_ref[0]], axis=0)
    ridx = lax.broadcasted_iota(jnp.int32, (rows, 1), 0)
    inside = ((ridx >= HALO) | (t > 0)) & ((ridx < HALO + tp) | (t < nt - 1))
    xn = jnp.where(inside, _rms(xa, nw_ref[...]), 0.0).astype(BF16)
    centre = slice(HALO, HALO + tp)

    def with_neighbours(z):
        return z[centre], pltpu.roll(z, 1, 0)[centre], pltpu.roll(z, rows - 1, 0)[centre]

    z_r = jnp.dot(xn, wr_ref[...], preferred_element_type=F32)
    _rwkv_prep_body(*with_neighbours(z_r), mu_ref, w0_ref, wup_ref, a0_ref, aup_ref, gup_ref, kk_ref, ka_ref,
                    seg_ref, trif_ref, trib_ref, pack_ref, g_ref, width=width)
    z_g = jnp.dot(xn, wg_ref[...], preferred_element_type=F32)
    qkv_w = 2 * kw + vw
    ggate_ref[0] = z_g[centre, qkv_w:qkv_w + vw].astype(ggate_ref.dtype)
    _gla_prep_body(*with_neighbours(z_g[:, :qkv_w]), z_g[centre, qkv_w + vw:qkv_w + vw + LANES],
                   cw_ref, gaup_ref, gab_ref, gpack_ref, la_ref, kw=kw)


def _front(x, nw, w_r, w_g, rwkv_consts, gla_consts, width, kw, vw):
    bsz, t_len, d = x.shape
    tp = ROW_TILE
    consts = (nw, w_r, w_g) + tuple(rwkv_consts) + tuple(gla_consts)
    big = lambda arr: arr.size * arr.dtype.itemsize >= (1 << 20)
    out = lambda w: pl.BlockSpec((1, tp, w), lambda b, t: (b, t, 0))
    shape = lambda w, dt: jax.ShapeDtypeStruct((bsz, t_len, w), dt)
    return pl.pallas_call(
        functools.partial(_front_kernel, width=width, kw=kw, vw=vw),
        grid=(bsz, t_len // tp),
        in_specs=_halo_specs(tp, d, t_len) + [_const_spec(c, 2, single_buffer=big(c)) for c in consts],
        out_specs=[out(6 * width), out(2 * width), out(2 * kw + vw), out(2 * kw), out(vw)],
        out_shape=[shape(6 * width, BF16), shape(2 * width, F32), shape(2 * kw + vw, BF16),
                   shape(2 * kw, F32), shape(vw, BF16)],
        compiler_params=pltpu.CompilerParams(dimension_semantics=("parallel", "parallel"),
                                             vmem_limit_bytes=VMEM_LIMIT),
        name="front",
    )(x, x, x, *consts)


def _rwkv_prep_body(z, z_dn, z_up, mu_ref, w0_ref, wup_ref, a0_ref, aup_ref, gup_ref,
                    kk_ref, ka_ref, seg_ref, trif_ref, trib_ref, pack_ref, g_ref, *, width):
    w = width
    zs = z + mu_ref[...] * (0.5 * (z_dn + z_up) - z)
    r = zs[:, 0:w]
    k = zs[:, w:2 * w]
    v = zs[:, 2 * w:3 * w]
    w_lo = zs[:, 3 * w:3 * w + LANES]
    a_lo = zs[:, 3 * w + LANES:3 * w + 2 * LANES]
    g_lo = zs[:, 3 * w + 2 * LANES:3 * w + 3 * LANES]
    w_log = -_softplus(-(w0_ref[...] + _mm(jnp.tanh(w_lo), wup_ref[...]))) - 0.5
    lw = -jnp.exp(w_log)
    g_ref[0, :, 0:w] = _mm_split_lhs(trif_ref[...], lw[:, 0:w])
    g_ref[0, :, w:2 * w] = _mm_split_lhs(trib_ref[...], lw[:, w:2 * w])
    a = _sigmoid(a0_ref[...] + _mm(a_lo, aup_ref[...]))
    g = _mm(_sigmoid(g_lo), gup_ref[...])
    kk = k * kk_ref[...]
    ss = _mm(kk * kk, seg_ref[...])
    kk = kk * lax.rsqrt(jnp.maximum(ss, 1e-24))
    k = k * (1.0 + (a - 1.0) * ka_ref[...])
    for j, val in enumerate((r, k, v, kk, kk * a, g)):
        pack_ref[0, :, j * w:(j + 1) * w] = val.astype(pack_ref.dtype)


def _chunk_tri_tiles():
    chunk_id = jnp.arange(ROW_TILE) // CHUNK
    pos = jnp.arange(ROW_TILE)
    same = chunk_id[:, None] == chunk_id[None, :]
    tri_f = (same & (pos[None, :] <= pos[:, None])).astype(BF16)
    tri_b = (same & (pos[None, :] >= pos[:, None])).astype(BF16)
    return tri_f, tri_b


def _delta_chunk_problems(refs, row, reverse, out_ref, slot0):
    r_ref, k_ref, v_ref, kk_ref, b_ref, g_ref = refs
    c = CHUNK
    g = g_ref[row]
    trow = lax.broadcasted_iota(jnp.int32, (c, 1), 0)
    if reverse:
        total = g[0:1, :]
        g_excl = jnp.where(trow == c - 1, 0.0, pltpu.roll(g, c - 1, 0))
    else:
        total = g[c - 1:c, :]
        g_excl = jnp.where(trow == 0, 0.0, pltpu.roll(g, 1, 0))
    einv = jnp.exp(-g)
    ec = jnp.exp(total - g)
    kk = kk_ref[row].astype(F32)
    bb = b_ref[row].astype(F32)
    kx = k_ref[row].astype(F32)
    full = dict(a=(-kk * jnp.exp(g_excl)).astype(BF16), r=(r_ref[row].astype(F32) * jnp.exp(g)).astype(BF16),
                bt=(bb * einv).astype(BF16), kt=(kx * einv).astype(BF16), v=v_ref[row],
                bh=(bb * ec).astype(BF16), kh=(kx * ec).astype(BF16))
    head0 = lax.broadcasted_iota(jnp.int32, (c, LANES), 1) < RWKV_HEAD_DIM
    tt = lax.broadcasted_iota(jnp.int32, (c, LANES), 0)
    ss = lax.broadcasted_iota(jnp.int32, (c, LANES), 1) % c
    incl = (ss >= tt) if reverse else (ss <= tt)
    strict = (ss > tt) if reverse else (ss < tt)
    probs = []
    for p in range(g.shape[1] // LANES):
        sl = slice(LANES * p, LANES * (p + 1))
        q = {name: z[:, sl] for name, z in full.items()}
        gam_col = jnp.exp(jnp.transpose(jnp.broadcast_to(total[:, sl], (LANES, LANES))))
        q.update(head0=head0, incl=incl, strict=strict, gam_col=gam_col, lanes=sl, row=row, out=out_ref,
                 slot=slot0 + p)
        probs.append(q)
    return probs


def _rwkv_scan_kernel(*refs, rows):
    in_f, in_b = refs[0:6], refs[6:12]
    yf_ref, yb_ref, h_ref = refs[12:15]

    @pl.when(pl.program_id(1) == 0)
    def _():
        h_ref[...] = jnp.zeros_like(h_ref)

    c = CHUNK
    n = LANES
    n_pairs = yf_ref.shape[-1] // LANES
    probs = []
    for row in range(rows):
        probs += _delta_chunk_problems(in_f, row, False, yf_ref, (2 * row) * n_pairs)
        probs += _delta_chunk_problems(in_b, row, True, yb_ref, (2 * row + 1) * n_pairs)
    stack = lambda z, q: _pair_stack(z.astype(BF16), q["head0"])
    rr = lax.broadcasted_iota(jnp.int32, (n, n), 0) // c
    cc = lax.broadcasted_iota(jnp.int32, (n, n), 1) // c
    same_head = rr == cc
    zero = jnp.zeros((c, n), F32)
    ar = [jnp.concatenate([q["a"], q["r"]], axis=0) for q in probs]
    gm = [_mm_nt(x, jnp.concatenate([stack(q["bt"], q), stack(q["kt"], q)], axis=0)) for x, q in zip(ar, probs)]
    hs = [h_ref[q["slot"]] for q in probs]
    xh = [_mm(x, h) for x, h in zip(ar, hs)]
    a_ab = [jnp.where(q["strict"], m[:c, :n], zero) for q, m in zip(probs, gm)]
    a_ak = [jnp.where(q["strict"], m[:c, n:], zero) for q, m in zip(probs, gm)]
    a_rbk = [jnp.where(jnp.concatenate([q["incl"], q["incl"]], axis=1), m[c:], 0.0) for q, m in zip(probs, gm)]
    vs = [stack(q["v"], q) for q in probs]
    zs = [x[:c] + _mm(a, v) for x, a, v in zip(xh, a_ak, vs)]
    ps = a_ab
    levels = CHUNK.bit_length() - 1
    for level in range(levels - 1):
        prods = [_mm(p, jnp.concatenate([stack(z, q), stack(p, q)], axis=1)) for p, z, q in zip(ps, zs, probs)]
        zs = [z + pr[:, :n] for z, pr in zip(zs, prods)]
        ps = [pr[:, n:] for pr in prods]
    us = [z + _mm(p, stack(z, q)) for p, z, q in zip(ps, zs, probs)]
    ys = [x[c:] + _mm(a, jnp.concatenate([stack(u, q), v], axis=0))
          for x, a, u, v, q in zip(xh, a_rbk, us, vs, probs)]
    upd = [_mm_tn(jnp.concatenate([q["bh"], q["kh"]], axis=0), jnp.concatenate([u.astype(BF16), q["v"]], axis=0))
           for q, u in zip(probs, us)]
    for q, y, h, d in zip(probs, ys, hs, upd):
        q["out"][q["row"], :, q["lanes"]] = y
        h_ref[q["slot"]] = h * q["gam_col"] + jnp.where(same_head, d, 0.0)


def _rwkv_scan(pack, g, width, rows):
    bsz, t_len, _ = pack.shape
    nc = t_len // CHUNK
    fwd = lambda j: pl.BlockSpec((rows, CHUNK, width), lambda b, t: (b, t, j))
    bwd = lambda j: pl.BlockSpec((rows, CHUNK, width), lambda b, t: (b, nc - 1 - t, j))
    n_pairs = width // LANES
    y_shape = jax.ShapeDtypeStruct((bsz, t_len, width), F32)
    return pl.pallas_call(
        functools.partial(_rwkv_scan_kernel, rows=rows),
        grid=(bsz // rows, nc),
        in_specs=[fwd(j) for j in (0, 1, 2, 3, 4)] + [fwd(0)] + [bwd(j) for j in (0, 1, 2, 3, 4)] + [bwd(1)],
        out_specs=[fwd(0), bwd(0)],
        out_shape=[y_shape, y_shape],
        scratch_shapes=[pltpu.VMEM((2 * rows * n_pairs, LANES, LANES), F32)],
        compiler_params=pltpu.CompilerParams(dimension_semantics=("parallel", "arbitrary"),
                                             vmem_limit_bytes=VMEM_LIMIT),
        name="rwkv_scan",
    )(*([pack] * 5 + [g] + [pack] * 5 + [g]))


def _gla_prep_body(x, x_dn, x_up, a_lo, cw_ref, aup_ref, ab_ref, pack_ref, la_ref, *, kw):
    qkv_w = x.shape[1]
    qkv = _silu(cw_ref[0:1, :] * x_dn + cw_ref[1:2, :] * x + cw_ref[2:3, :] * x_up)
    logit = _mm(a_lo, aup_ref[...]) + ab_ref[...]
    la_ref[0] = -_softplus(-logit) * (1.0 / GLA_GATE_TEMP)
    dk = kw // GLA_HEADS
    pack_ref[0, :, 0:kw] = (qkv[:, 0:kw] * (dk ** -0.5)).astype(pack_ref.dtype)
    pack_ref[0, :, kw:qkv_w] = qkv[:, kw:qkv_w].astype(pack_ref.dtype)


def _gla_block_operands(refs, reverse, slot0, out_ref):
    q_ref, k_ref, v_ref, la_ref = refs
    c = CHUNK
    nb = la_ref.shape[1] // c
    tri = _chunk_causal(reverse)
    tri_bf = tri.astype(BF16)
    head0 = lax.broadcasted_iota(jnp.int32, (c, LANES), 1) < (LANES // 2)
    zero = jnp.zeros((c, LANES), F32)
    probs = []
    order = range(nb - 1, -1, -1) if reverse else range(nb)
    for step, ci in enumerate(order):
        rows = slice(ci * c, (ci + 1) * c)
        la = la_ref[0, rows, :]
        bc = _mm_exact_lhs(tri_bf, la)
        total = jnp.sum(la, axis=0, keepdims=True)
        kx = k_ref[0, rows, :].astype(F32)
        q_t = q_ref[0, rows, :].astype(F32) * jnp.exp(bc)
        k_t = kx * jnp.exp(-bc)
        k_h = kx * jnp.exp(total - bc)
        gam = jnp.exp(total)
        for p in range(la.shape[1] // LANES):
            sl = slice(LANES * p, LANES * (p + 1))
            for h in range(2):
                hm = head0 if h == 0 else jnp.logical_not(head0)
                vsl = slice(LANES * (2 * p + h), LANES * (2 * p + h + 1))
                probs.append(dict(q=jnp.where(hm, q_t[:, sl], zero), kt=k_t[:, sl],
                                  kh=jnp.where(hm, k_h[:, sl], zero), v=v_ref[0, rows, vsl], causal=tri,
                                  gam=gam[:, sl], slot=slot0 + p, step=step, rows=rows, lanes=vsl,
                                  out=out_ref))
    return probs


def _gla_scan_kernel(*refs):
    in_f, in_b = refs[0:4], refs[4:8]
    of_ref, ob_ref, st_ref = refs[8:11]

    @pl.when(pl.program_id(1) == 0)
    def _():
        st_ref[...] = jnp.zeros_like(st_ref)

    n_slots = st_ref.shape[0]
    probs = (_gla_block_operands(in_f, False, 0, of_ref)
             + _gla_block_operands(in_b, True, n_slots // 2, ob_ref))
    n_steps = 1 + max(q["step"] for q in probs)
    scores = [jnp.where(q["causal"], _mm_nt(q["q"], q["kt"]), 0.0) for q in probs]
    intra = [_mm(s, q["v"]) for s, q in zip(scores, probs)]
    upd = [_mm_tn(q["v"], q["kh"]) for q in probs]
    state = [st_ref[s] for s in range(n_slots)]
    entering = [None] * len(probs)
    for step in range(n_steps):
        for s in range(n_slots):
            mine = [i for i, q in enumerate(probs) if q["slot"] == s and q["step"] == step]
            for i in mine:
                entering[i] = state[s]
            state[s] = state[s] * probs[mine[0]]["gam"] + upd[mine[0]] + upd[mine[1]]
    inter = [_mm_nt(q["q"], st) for q, st in zip(probs, entering)]
    for i, q in enumerate(probs):
        q["out"][0, q["rows"], q["lanes"]] = intra[i] + inter[i]
    for s in range(n_slots):
        st_ref[s] = state[s]


def _gla_scan(pack, la, kw, vw):
    bsz, t_len, _ = pack.shape
    rows = GLA_BLOCK_CHUNKS * CHUNK
    nb = t_len // rows
    fwd = lambda w, j: pl.BlockSpec((1, rows, w), lambda b, t: (b, t, j))
    bwd = lambda w, j: pl.BlockSpec((1, rows, w), lambda b, t: (b, nb - 1 - t, j))
    o_shape = jax.ShapeDtypeStruct((bsz, t_len, vw), F32)
    specs = lambda blk, la_col: [blk(kw, 0), blk(kw, 1), blk(vw, (2 * kw) // vw), blk(kw, la_col)]
    return pl.pallas_call(
        _gla_scan_kernel,
        grid=(bsz, nb),
        in_specs=specs(fwd, 0) + specs(bwd, 1),
        out_specs=[fwd(vw, 0), bwd(vw, 0)],
        out_shape=[o_shape, o_shape],
        scratch_shapes=[pltpu.VMEM((2 * (kw // LANES), LANES, LANES), F32)],
        compiler_params=pltpu.CompilerParams(dimension_semantics=("parallel", "arbitrary"),
                                             vmem_limit_bytes=VMEM_LIMIT),
        name="gla_scan",
    )(*([pack] * 3 + [la] + [pack] * 3 + [la]))


def _tail_kernel(x_ref, yf_ref, yb_ref, r_ref, k_ref, v_ref, g_ref, of_ref, ob_ref, zg_ref, p_ref,
                 seg_ref, rk_ref, lnw_ref, lnb_ref, gn_ref, wor_ref, wog_ref, nmix_ref,
                 nfpre_ref, nfpost_ref, wgate_ref, wup_ref, wdown_ref, wpp_ref, wpg_ref, bpg_ref, nple_ref,
                 out_ref):
    dot = functools.partial(jnp.dot, preferred_element_type=F32)
    seg = seg_ref[...]
    inv_n = 1.0 / RWKV_HEAD_DIM
    y = yf_ref[...] + yb_ref[...]
    mu = _mm(y, seg) * inv_n
    yc = y - mu
    var = _mm(yc * yc, seg) * inv_n
    yn = yc * lax.rsqrt(var + RWKV_GN_EPS) * lnw_ref[...] + lnb_ref[...]
    r = r_ref[...].astype(F32)
    bonus = _mm(r * k_ref[...].astype(F32) * rk_ref[...], seg) * v_ref[...].astype(F32)
    y_r = ((yn + bonus) * g_ref[...].astype(F32)).astype(BF16)
    o = of_ref[...] + ob_ref[...]
    gate_g = zg_ref[...].astype(F32)
    y_g = jnp.concatenate(
        [_rms(o[:, LANES * h:LANES * (h + 1)], gn_ref[...]) * _silu(gate_g[:, LANES * h:LANES * (h + 1)])
         for h in range(GLA_HEADS)], axis=1).astype(BF16)
    h1 = x_ref[...] + _rms(dot(y_r, wor_ref[...]) + dot(y_g, wog_ref[...]), nmix_ref[...])
    hn = _rms(h1, nfpre_ref[...]).astype(BF16)
    act = (_silu(dot(hn, wgate_ref[...])) * dot(hn, wup_ref[...])).astype(BF16)
    h2 = h1 + _rms(dot(act, wdown_ref[...]), nfpost_ref[...])
    e = dot(p_ref[...].astype(BF16), wpp_ref[...])
    gate = _sigmoid(dot(h2.astype(BF16), wpg_ref[...]) + bpg_ref[...])
    out_ref[...] = h2 + _rms(gate * e, nple_ref[...])


def _tail(x2, y_f, y_b, pack, o_f, o_b, ggate, p2, consts):
    m, d = x2.shape
    tm = ROW_TILE
    w = y_f.shape[1]
    row = lambda width, j=0: pl.BlockSpec((tm, width), lambda i: (i, j))
    big = lambda arr: arr.size * arr.dtype.itemsize >= (1 << 20)
    return pl.pallas_call(
        _tail_kernel,
        grid=(m // tm,),
        in_specs=[row(d), row(w), row(w), row(w, 0), row(w, 1), row(w, 2), row(w, 5), row(w), row(w),
                  row(w), row(p2.shape[1])] + [_const_spec(c, 1, single_buffer=big(c)) for c in consts],
        out_specs=row(d),
        out_shape=jax.ShapeDtypeStruct((m, d), F32),
        compiler_params=pltpu.CompilerParams(dimension_semantics=("parallel",),
                                             vmem_limit_bytes=VMEM_LIMIT),
        name="tail",
    )(x2, y_f, y_b, pack, pack, pack, pack, o_f, o_b, ggate, p2, *consts)


def _pad_cols(w, n):
    return jnp.pad(w, ((0, 0), (0, n - w.shape[1])))


def _layer(h, p_i, norm_mix_pre, norm_mix_post, norm_ffn_pre, norm_ffn_post, norm_ple,
           w_in, rwkv_mu, rwkv_w0, rwkv_w_up, rwkv_a0, rwkv_a_up, rwkv_g_up,
           rwkv_k_k, rwkv_k_a, rwkv_r_k, rwkv_ln_w, rwkv_ln_b,
           gla_conv, gla_a_up, gla_a_b, gla_norm, w_out,
           ffn_gate, ffn_up, ffn_down, ple_proj, ple_gate, ple_gate_b):
    bsz, t_len, d = h.shape
    m = bsz * t_len
    rw_w = rwkv_w0.shape[1]
    kw = gla_a_b.shape[1]
    vw = w_out.shape[0] - rw_w
    assert vw == rw_w and t_len % (GLA_BLOCK_CHUNKS * CHUNK) == 0 and m % ROW_TILE == 0
    rwkv_in = 3 * rw_w + 2 * RWKV_DECAY_LORA + RWKV_AAA_LORA + RWKV_GATE_LORA
    row = lambda vec: vec.reshape(1, -1)

    lo_w = 3 * rw_w + 2 * RWKV_DECAY_LORA
    a_end = lo_w + RWKV_AAA_LORA

    def rwkv_cols(wmat):
        return jnp.concatenate([wmat[:, :lo_w], _pad_cols(wmat[:, lo_w:a_end], LANES),
                                wmat[:, a_end:rwkv_in]], axis=1)

    w_r = rwkv_cols(w_in[:, :rwkv_in]).astype(BF16)
    mu = rwkv_cols(row(rwkv_mu))
    gla_in = w_in.shape[1] - rwkv_in
    gla_main = gla_in - 2 * GLA_GATE_LORA
    w_g = jnp.concatenate([w_in[:, rwkv_in:rwkv_in + gla_main],
                           _pad_cols(w_in[:, rwkv_in + gla_main:], LANES)], axis=1).astype(BF16)

    zeros = jnp.zeros((RWKV_DECAY_LORA, rw_w), F32)
    w_up = jnp.concatenate([jnp.concatenate([rwkv_w_up[0], zeros], axis=1),
                            jnp.concatenate([zeros, rwkv_w_up[1]], axis=1)], axis=0).astype(BF16)
    a_up = jnp.pad(rwkv_a_up, ((0, LANES - RWKV_AAA_LORA), (0, 0))).astype(BF16)
    g_up = rwkv_g_up.astype(BF16)
    gz = jnp.zeros((GLA_GATE_LORA, kw), F32)
    gla_up = jnp.concatenate([jnp.concatenate([gla_a_up[0], gz], axis=1),
                              jnp.concatenate([gz, gla_a_up[1]], axis=1)], axis=0)
    gla_up = jnp.pad(gla_up, ((0, LANES - 2 * GLA_GATE_LORA), (0, 0))).astype(BF16)
    head_id = jnp.arange(rw_w) // RWKV_HEAD_DIM
    seg = (head_id[:, None] == head_id[None, :]).astype(BF16)

    x2 = h.reshape(m, d)
    rwkv_consts = (mu, rwkv_w0.reshape(1, -1), w_up, row(rwkv_a0), a_up, g_up, row(rwkv_k_k), row(rwkv_k_a),
                   seg) + _chunk_tri_tiles()
    gla_consts = (gla_conv.reshape(gla_conv.shape[0], -1), gla_up, gla_a_b.reshape(1, -1))
    pack, lw, gpack, la, ggate = _front(h, row(norm_mix_pre), w_r, w_g, rwkv_consts, gla_consts, rw_w, kw, vw)
    y_f, y_b = _rwkv_scan(pack, lw, rw_w, 2 if bsz % 2 == 0 else 1)
    o_f, o_b = _gla_scan(gpack, la, kw, vw)

    w_o = w_out.astype(BF16)
    consts = (seg, row(rwkv_r_k), row(rwkv_ln_w), row(rwkv_ln_b), row(gla_norm), w_o[:rw_w], w_o[rw_w:],
              row(norm_mix_post), row(norm_ffn_pre), row(norm_ffn_post), ffn_gate.astype(BF16),
              ffn_up.astype(BF16), ffn_down.astype(BF16), ple_proj.astype(BF16), ple_gate.astype(BF16),
              row(ple_gate_b), row(norm_ple))
    out = _tail(x2, y_f.reshape(m, rw_w), y_b.reshape(m, rw_w), pack.reshape(m, -1), o_f.reshape(m, vw),
                o_b.reshape(m, vw), ggate.reshape(m, vw), p_i.reshape(m, -1), consts)
    return out.reshape(bsz, t_len, d)


def kernel(x, p, norm_mix_pre, norm_mix_post, norm_ffn_pre, norm_ffn_post, norm_ple, w_in, rwkv_mu, rwkv_w0, rwkv_w_up, rwkv_a0, rwkv_a_up, rwkv_g_up, rwkv_k_k, rwkv_k_a, rwkv_r_k, rwkv_ln_w, rwkv_ln_b, gla_conv, gla_a_up, gla_a_b, gla_norm, w_out, ffn_gate, ffn_up, ffn_down, ple_proj, ple_gate, ple_gate_b):
    params = (norm_mix_pre, norm_mix_post, norm_ffn_pre, norm_ffn_post, norm_ple, w_in, rwkv_mu,
              rwkv_w0, rwkv_w_up, rwkv_a0, rwkv_a_up, rwkv_g_up, rwkv_k_k, rwkv_k_a, rwkv_r_k,
              rwkv_ln_w, rwkv_ln_b, gla_conv, gla_a_up, gla_a_b, gla_norm, w_out, ffn_gate, ffn_up,
              ffn_down, ple_proj, ple_gate, ple_gate_b)
    h = x
    for i in range(p.shape[0]):
        h = _layer(h, p[i], *(w[i] for w in params))
    return h
```

```python
import functools

import jax
import jax.numpy as jnp
from jax import lax
from jax.experimental import pallas as pl
from jax.experimental.pallas import tpu as pltpu

F32 = jnp.float32
BF16 = jnp.bfloat16

NORM_EPS = 1e-6
RWKV_GN_EPS = 64e-5
RWKV_HEAD_DIM = 64
RWKV_DECAY_LORA = 64
RWKV_AAA_LORA = 64
RWKV_GATE_LORA = 128
GLA_HEADS = 4
GLA_GATE_LORA = 16
GLA_GATE_TEMP = 16.0
DECAY_SCALE = 0.6065306597126334
CHUNK = 64
RWKV_SCAN_ROWS = 4
GLA_BLOCK_CHUNKS = 4
GLA_FACTORISED_LOG_RANGE = 60.0
LANES = 128
VMEM_LIMIT = 56 * 1024 * 1024
ROW_TILE = 256


def _mm(a, b):
    return jnp.dot(a.astype(BF16), b.astype(BF16), preferred_element_type=F32)


def _mm_nt(a, b):
    return lax.dot_general(a.astype(BF16), b.astype(BF16), (((1,), (1,)), ((), ())),
                           preferred_element_type=F32)


def _mm_tn(a, b):
    return lax.dot_general(a.astype(BF16), b.astype(BF16), (((0,), (0,)), ((), ())),
                           preferred_element_type=F32)


def _mm_split_lhs(a_bf16, x):
    hi = x.astype(BF16)
    lo = (x - hi.astype(F32)).astype(BF16)
    d = functools.partial(jnp.dot, preferred_element_type=F32)
    return d(a_bf16, hi) + d(a_bf16, lo)


def _rms(x, w):
    ms = jnp.mean(x * x, axis=-1, keepdims=True)
    return x * lax.rsqrt(ms + NORM_EPS) * w


def _softplus(u):
    return jnp.maximum(u, 0.0) + jnp.log(1.0 + jnp.exp(-jnp.abs(u)))


def _sigmoid(u):
    return 1.0 / (1.0 + jnp.exp(-u))


def _silu(u):
    return u * _sigmoid(u)


HALO = 8


def _halo_specs(tp, width, t_len):
    n_halo_blocks = t_len // HALO
    per_tile = tp // HALO
    main = pl.BlockSpec((1, tp, width), lambda b, t: (b, t, 0))
    prev = pl.BlockSpec((1, HALO, width), lambda b, t: (b, jnp.maximum(t * per_tile - 1, 0), 0))
    nxt = pl.BlockSpec((1, HALO, width), lambda b, t: (b, jnp.minimum((t + 1) * per_tile, n_halo_blocks - 1), 0))
    return [main, prev, nxt]


def _pair_stack(z, first_half):
    zero = jnp.zeros_like(z)
    return jnp.concatenate([jnp.where(first_half, z, zero), jnp.where(first_half, zero, z)], axis=0)


def _chunk_causal(reverse):
    r = lax.broadcasted_iota(jnp.int32, (CHUNK, CHUNK), 0)
    c = lax.broadcasted_iota(jnp.int32, (CHUNK, CHUNK), 1)
    return (c >= r) if reverse else (c <= r)


def _const_spec(arr, grid_rank, single_buffer=False):
    idx = (lambda i: (0,) * arr.ndim) if grid_rank == 1 else (lambda b, t: (0,) * arr.ndim)
    if single_buffer:
        return pl.BlockSpec(arr.shape, idx, pipeline_mode=pl.Buffered(1))
    return pl.BlockSpec(arr.shape, idx)


def _front_kernel(x_ref, xp_ref, xn_ref, nw_ref, wr_ref, wg_ref,
                  mu_ref, w0_ref, wup_ref, a0_ref, aup_ref, gup_ref, kk_ref, ka_ref, seg_ref, trif_ref, trib_ref,
                  cw_ref, gaup_ref, gab_ref,
                  pack_ref, g_ref, gpack_ref, la_ref, ggate_ref, *, width, kw, vw):
    t = pl.program_id(1)
    nt = pl.num_programs(1)
    tp = x_ref.shape[1]
    rows = tp + 2 * HALO
    xa = jnp.concatenate([xp_ref[0], x_ref[0], xn_ref[0]], axis=0)
    ridx = lax.broadcasted_iota(jnp.int32, (rows, 1), 0)
    inside = ((ridx >= HALO) | (t > 0)) & ((ridx < HALO + tp) | (t < nt - 1))
    xn = jnp.where(inside, _rms(xa, nw_ref[...]), 0.0).astype(BF16)
    centre = slice(HALO, HALO + tp)

    def with_neighbours(z):
        return z[centre], pltpu.roll(z, 1, 0)[centre], pltpu.roll(z, rows - 1, 0)[centre]

    z_r = jnp.dot(xn, wr_ref[...], preferred_element_type=F32)
    z_g = jnp.dot(xn, wg_ref[...], preferred_element_type=F32)
    _rwkv_prep_body(*with_neighbours(z_r), mu_ref, w0_ref, wup_ref, a0_ref, aup_ref, gup_ref, kk_ref, ka_ref,
                    seg_ref, trif_ref, trib_ref, pack_ref, g_ref, width=width)
    qkv_w = 2 * kw + vw
    ggate_ref[0] = z_g[centre, qkv_w:qkv_w + vw].astype(ggate_ref.dtype)
    _gla_prep_body(*with_neighbours(z_g[:, :qkv_w]), z_g[centre, qkv_w + vw:qkv_w + vw + LANES],
                   cw_ref, gaup_ref, gab_ref, gpack_ref, la_ref, kw=kw)


def _front(x, nw, w_r, w_g, rwkv_consts, gla_consts, width, kw, vw):
    bsz, t_len, d = x.shape
    tp = ROW_TILE
    consts = (nw, w_r, w_g) + tuple(rwkv_consts) + tuple(gla_consts)
    big = lambda arr: arr.size * arr.dtype.itemsize >= (1 << 20)
    out = lambda w: pl.BlockSpec((1, tp, w), lambda b, t: (b, t, 0))
    shape = lambda w, dt: jax.ShapeDtypeStruct((bsz, t_len, w), dt)
    return pl.pallas_call(
        functools.partial(_front_kernel, width=width, kw=kw, vw=vw),
        grid=(bsz, t_len // tp),
        in_specs=_halo_specs(tp, d, t_len) + [_const_spec(c, 2, single_buffer=big(c)) for c in consts],
        out_specs=[out(6 * width), out(2 * width), out(2 * kw + vw), out(2 * kw), out(vw)],
        out_shape=[shape(6 * width, BF16), shape(2 * width, F32), shape(2 * kw + vw, BF16),
                   shape(2 * kw, F32), shape(vw, BF16)],
        compiler_params=pltpu.CompilerParams(dimension_semantics=("parallel", "parallel"),
                                             vmem_limit_bytes=VMEM_LIMIT),
        name="front",
    )(x, x, x, *consts)


def _rwkv_prep_body(z, z_dn, z_up, mu_ref, w0_ref, wup_ref, a0_ref, aup_ref, gup_ref,
                    kk_ref, ka_ref, seg_ref, trif_ref, trib_ref, pack_ref, g_ref, *, width):
    w = width
    zs = z + mu_ref[...] * (0.5 * (z_dn + z_up) - z)
    r = zs[:, 0:w]
    k = zs[:, w:2 * w]
    v = zs[:, 2 * w:3 * w]
    w_lo = zs[:, 3 * w:3 * w + LANES]
    a_lo = zs[:, 3 * w + LANES:3 * w + 2 * LANES]
    g_lo = zs[:, 3 * w + 2 * LANES:3 * w + 3 * LANES]
    lw = -DECAY_SCALE * _sigmoid(w0_ref[...] + _mm(jnp.tanh(w_lo), wup_ref[...]))
    g_ref[0, :, 0:w] = _mm_split_lhs(trif_ref[...], lw[:, 0:w])
    g_ref[0, :, w:2 * w] = _mm_split_lhs(trib_ref[...], lw[:, w:2 * w])
    a = _sigmoid(a0_ref[...] + _mm(a_lo, aup_ref[...]))
    g = _mm(_sigmoid(g_lo), gup_ref[...])
    kk = k * kk_ref[...]
    ss = _mm(kk * kk, seg_ref[...])
    kk = kk * lax.rsqrt(jnp.maximum(ss, 1e-24))
    k = k * (1.0 + (a - 1.0) * ka_ref[...])
    for j, val in enumerate((r, k, v, kk, kk * a, g)):
        pack_ref[0, :, j * w:(j + 1) * w] = val.astype(pack_ref.dtype)


def _chunk_tri_tiles():
    chunk_id = jnp.arange(ROW_TILE) // CHUNK
    pos = jnp.arange(ROW_TILE)
    same = chunk_id[:, None] == chunk_id[None, :]
    tri_f = (same & (pos[None, :] <= pos[:, None])).astype(BF16)
    tri_b = (same & (pos[None, :] >= pos[:, None])).astype(BF16)
    return tri_f, tri_b


def _delta_chunk_problems(refs, row, reverse, out_ref, slot0):
    r_ref, k_ref, v_ref, kk_ref, b_ref, g_ref = refs
    c = CHUNK
    g = g_ref[row]
    trow = lax.broadcasted_iota(jnp.int32, (c, 1), 0)
    if reverse:
        total = g[0:1, :]
        g_excl = jnp.where(trow == c - 1, 0.0, pltpu.roll(g, c - 1, 0))
    else:
        total = g[c - 1:c, :]
        g_excl = jnp.where(trow == 0, 0.0, pltpu.roll(g, 1, 0))
    einv = jnp.exp(-g)
    ec = jnp.exp(total - g)
    kk = kk_ref[row].astype(F32)
    bb = b_ref[row].astype(F32)
    kx = k_ref[row].astype(F32)
    full = dict(a=(-kk * jnp.exp(g_excl)).astype(BF16), r=(r_ref[row].astype(F32) * jnp.exp(g)).astype(BF16),
                bt=(bb * einv).astype(BF16), kt=(kx * einv).astype(BF16), v=v_ref[row],
                bh=(bb * ec).astype(BF16), kh=(kx * ec).astype(BF16))
    head0 = lax.broadcasted_iota(jnp.int32, (c, LANES), 1) < RWKV_HEAD_DIM
    tt = lax.broadcasted_iota(jnp.int32, (c, LANES), 0)
    ss = lax.broadcasted_iota(jnp.int32, (c, LANES), 1) % c
    incl = (ss >= tt) if reverse else (ss <= tt)
    strict = (ss > tt) if reverse else (ss < tt)
    probs = []
    for p in range(total.shape[1] // LANES):
        sl = slice(LANES * p, LANES * (p + 1))
        q = {name: z[:, sl] for name, z in full.items()}
        gam_col = jnp.exp(jnp.transpose(jnp.broadcast_to(total[:, sl], (LANES, LANES))))
        q.update(head0=head0, incl=incl, strict=strict, gam_col=gam_col, lanes=sl, row=row, out=out_ref,
                 slot=slot0 + p)
        probs.append(q)
    return probs


def _rwkv_scan_kernel(*refs, rows):
    in_f, in_b = refs[0:6], refs[6:12]
    yf_ref, yb_ref, h_ref = refs[12:15]

    @pl.when(pl.program_id(1) == 0)
    def _():
        h_ref[...] = jnp.zeros_like(h_ref)

    c = CHUNK
    n = LANES
    n_pairs = yf_ref.shape[-1] // LANES
    probs = []
    for row in range(rows):
        probs += _delta_chunk_problems(in_f, row, False, yf_ref, (2 * row) * n_pairs)
        probs += _delta_chunk_problems(in_b, row, True, yb_ref, (2 * row + 1) * n_pairs)
    stack = lambda z, q: _pair_stack(z.astype(BF16), q["head0"])
    rr = lax.broadcasted_iota(jnp.int32, (n, n), 0) // c
    cc = lax.broadcasted_iota(jnp.int32, (n, n), 1) // c
    same_head = rr == cc
    zero = jnp.zeros((c, n), F32)
    ar = [jnp.concatenate([q["a"], q["r"]], axis=0) for q in probs]
    gm = [_mm_nt(x, jnp.concatenate([stack(q["bt"], q), stack(q["kt"], q)], axis=0)) for x, q in zip(ar, probs)]
    hs = [h_ref[q["slot"]] for q in probs]
    hb = [h.astype(BF16) for h in hs]
    rh = [_mm(q["r"], h) for q, h in zip(probs, hb)]
    a_ab = [jnp.where(q["strict"], m[:c, :n], zero) for q, m in zip(probs, gm)]
    a_ak = [jnp.where(q["strict"], m[:c, n:], zero) for q, m in zip(probs, gm)]
    a_rbk = [jnp.where(jnp.concatenate([q["incl"], q["incl"]], axis=1), m[c:], 0.0) for q, m in zip(probs, gm)]
    vs = [stack(q["v"], q) for q in probs]
    zs = [_mm(jnp.concatenate([q["a"], a.astype(BF16)], axis=1), jnp.concatenate([h, v], axis=0))
          for q, a, h, v in zip(probs, a_ak, hb, vs)]
    ps = a_ab
    levels = CHUNK.bit_length() - 1
    for level in range(levels - 1):
        prods = [_mm(p, jnp.concatenate([stack(z, q), stack(p, q)], axis=1)) for p, z, q in zip(ps, zs, probs)]
        zs = [z + pr[:, :n] for z, pr in zip(zs, prods)]
        ps = [pr[:, n:] for pr in prods]
    us = [z + _mm(p, stack(z, q)) for p, z, q in zip(ps, zs, probs)]
    ys = [x + _mm(a, jnp.concatenate([stack(u, q), v], axis=0))
          for x, a, u, v, q in zip(rh, a_rbk, us, vs, probs)]
    upd = [_mm_tn(jnp.concatenate([q["bh"], q["kh"]], axis=0), jnp.concatenate([u.astype(BF16), q["v"]], axis=0))
           for q, u in zip(probs, us)]
    for q, y, h, d in zip(probs, ys, hs, upd):
        q["out"][q["row"], :, q["lanes"]] = y
        h_ref[q["slot"]] = h * q["gam_col"] + jnp.where(same_head, d, 0.0)


def _rwkv_scan(pack, g, width, rows):
    bsz, t_len, _ = pack.shape
    nc = t_len // CHUNK
    fwd = lambda j: pl.BlockSpec((rows, CHUNK, width), lambda b, t: (b, t, j))
    bwd = lambda j: pl.BlockSpec((rows, CHUNK, width), lambda b, t: (b, nc - 1 - t, j))
    n_pairs = width // LANES
    y_shape = jax.ShapeDtypeStruct((bsz, t_len, width), F32)
    return pl.pallas_call(
        functools.partial(_rwkv_scan_kernel, rows=rows),
        grid=(bsz // rows, nc),
        in_specs=[fwd(j) for j in (0, 1, 2, 3, 4)] + [fwd(0)] + [bwd(j) for j in (0, 1, 2, 3, 4)] + [bwd(1)],
        out_specs=[fwd(0), bwd(0)],
        out_shape=[y_shape, y_shape],
        scratch_shapes=[pltpu.VMEM((2 * rows * n_pairs, LANES, LANES), F32)],
        compiler_params=pltpu.CompilerParams(dimension_semantics=("parallel", "arbitrary"),
                                             vmem_limit_bytes=VMEM_LIMIT),
        name="rwkv_scan",
    )(*([pack] * 5 + [g] + [pack] * 5 + [g]))


def _gla_prep_body(x, x_dn, x_up, a_lo, cw_ref, aup_ref, ab_ref, pack_ref, la_ref, *, kw):
    qkv_w = x.shape[1]
    qkv = _silu(cw_ref[0:1, :] * x_dn + cw_ref[1:2, :] * x + cw_ref[2:3, :] * x_up)
    logit = _mm(a_lo, aup_ref[...]) + ab_ref[...]
    la_ref[0] = -_softplus(-logit) * (1.0 / GLA_GATE_TEMP)
    dk = kw // GLA_HEADS
    pack_ref[0, :, 0:kw] = (qkv[:, 0:kw] * (dk ** -0.5)).astype(pack_ref.dtype)
    pack_ref[0, :, kw:qkv_w] = qkv[:, kw:qkv_w].astype(pack_ref.dtype)


def _gla_block_problems(refs, reverse, slot0, out_ref):
    q_ref, k_ref, v_ref, la_ref = refs
    c = CHUNK
    nb = la_ref.shape[1] // c
    tri_bf = _chunk_causal(reverse).astype(BF16)
    tt = lax.broadcasted_iota(jnp.int32, (c, LANES), 0)
    ss = lax.broadcasted_iota(jnp.int32, (c, LANES), 1) % c
    causal = (ss >= tt) if reverse else (ss <= tt)
    probs = []
    order = range(nb - 1, -1, -1) if reverse else range(nb)
    for step, ci in enumerate(order):
        rows = slice(ci * c, (ci + 1) * c)
        la = la_ref[0, rows, :]
        bc = _mm_split_lhs(tri_bf, la)
        total = bc[0:1, :] if reverse else bc[c - 1:c, :]
        kx = k_ref[0, rows, :].astype(F32)
        q_t = (q_ref[0, rows, :].astype(F32) * jnp.exp(bc)).astype(BF16)
        k_t = (kx * jnp.exp(-bc)).astype(BF16)
        k_h = (kx * jnp.exp(total - bc)).astype(BF16)
        lowest = jnp.min(total)
        for p in range(la.shape[1] // LANES):
            sl = slice(LANES * p, LANES * (p + 1))
            vsl = slice(2 * LANES * p, 2 * LANES * (p + 1))
            gam = jnp.exp(jnp.transpose(jnp.broadcast_to(total[:, sl], (LANES, LANES))))
            probs.append(dict(q=q_t[:, sl], kt=k_t[:, sl], kh=k_h[:, sl], v=v_ref[0, rows, vsl], causal=causal,
                              gam_col=jnp.concatenate([gam, gam], axis=1), slot=slot0 + p, step=step, rows=rows,
                              lanes=vsl, out=out_ref, reverse=reverse, lowest=lowest, bc=bc[:, sl],
                              k_raw=kx[:, sl], q_ref=q_ref, key_lanes=sl))
    return probs


def _gla_intra_pairwise(q, inter, rob_ref):
    c = CHUNK
    bc = q["bc"]
    q_raw = q["q_ref"][0, q["rows"], q["key_lanes"]].astype(F32)
    rob_ref[0] = bc
    rob_ref[1] = q["k_raw"]
    rob_ref[2] = q["v"][:, :LANES].astype(F32)
    rob_ref[3] = q["v"][:, LANES:].astype(F32)
    trow = lax.broadcasted_iota(jnp.int32, (c, 1), 0)
    head0 = lax.broadcasted_iota(jnp.int32, (c, LANES), 1) < (LANES // 2)

    def body(j, acc):
        acc0, acc1 = acc
        b_j = rob_ref[0, pl.ds(j, 1), :]
        k_j = rob_ref[1, pl.ds(j, 1), :]
        visible = (trow <= j) if q["reverse"] else (trow >= j)
        pj = q_raw * k_j * jnp.exp(jnp.where(visible, bc - b_j, -jnp.inf))
        s0 = jnp.sum(jnp.where(head0, pj, 0.0), axis=1, keepdims=True)
        s1 = jnp.sum(jnp.where(head0, 0.0, pj), axis=1, keepdims=True)
        return acc0 + s0 * rob_ref[2, pl.ds(j, 1), :], acc1 + s1 * rob_ref[3, pl.ds(j, 1), :]

    zero = jnp.zeros((c, LANES), F32)
    acc0, acc1 = lax.fori_loop(0, c, body, (zero, zero))
    q["out"][0, q["rows"], q["lanes"]] = inter + jnp.concatenate([acc0, acc1], axis=1)


def _gla_scan_kernel(*refs):
    in_f, in_b = refs[0:4], refs[4:8]
    of_ref, ob_ref, st_ref, rob_ref = refs[8:12]

    @pl.when(pl.program_id(1) == 0)
    def _():
        st_ref[...] = jnp.zeros_like(st_ref)

    c = CHUNK
    n_slots = st_ref.shape[0]
    probs = (_gla_block_problems(in_f, False, 0, of_ref)
             + _gla_block_problems(in_b, True, n_slots // 2, ob_ref))
    n_steps = 1 + max(q["step"] for q in probs)
    head0_k = lax.broadcasted_iota(jnp.int32, (c, LANES), 1) < (LANES // 2)
    head0_v = lax.broadcasted_iota(jnp.int32, (c, 2 * LANES), 1) < LANES
    same_head = (lax.broadcasted_iota(jnp.int32, (LANES, 2 * LANES), 0) // (LANES // 2)
                 == lax.broadcasted_iota(jnp.int32, (LANES, 2 * LANES), 1) // LANES)
    scores = [jnp.where(q["causal"], _mm_nt(q["q"], _pair_stack(q["kt"], head0_k)), 0.0) for q in probs]
    upd = [jnp.where(same_head, _mm_tn(q["kh"], q["v"]), 0.0) for q in probs]
    state = [st_ref[s] for s in range(n_slots)]
    entering = [None] * len(probs)
    for step in range(n_steps):
        for i, q in enumerate(probs):
            if q["step"] == step:
                entering[i] = state[q["slot"]]
                state[q["slot"]] = state[q["slot"]] * q["gam_col"] + upd[i]
    outs = [_mm(jnp.concatenate([s.astype(BF16), q["q"]], axis=1),
                jnp.concatenate([_pair_stack(q["v"], head0_v), st.astype(BF16)], axis=0))
            for s, q, st in zip(scores, probs, entering)]
    for q, o in zip(probs, outs):
        q["out"][0, q["rows"], q["lanes"]] = o
    for s in range(n_slots):
        st_ref[s] = state[s]

    lowest = functools.reduce(jnp.minimum, [q["lowest"] for q in probs])

    @pl.when(lowest < -GLA_FACTORISED_LOG_RANGE)
    def _():
        for q, st in zip(probs, entering):
            _gla_intra_pairwise(q, _mm(q["q"], st), rob_ref)


def _gla_scan(pack, la, kw, vw):
    bsz, t_len, _ = pack.shape
    rows = GLA_BLOCK_CHUNKS * CHUNK
    nb = t_len // rows
    fwd = lambda w, j: pl.BlockSpec((1, rows, w), lambda b, t: (b, t, j))
    bwd = lambda w, j: pl.BlockSpec((1, rows, w), lambda b, t: (b, nb - 1 - t, j))
    o_shape = jax.ShapeDtypeStruct((bsz, t_len, vw), F32)
    specs = lambda blk, la_col: [blk(kw, 0), blk(kw, 1), blk(vw, (2 * kw) // vw), blk(kw, la_col)]
    return pl.pallas_call(
        _gla_scan_kernel,
        grid=(bsz, nb),
        in_specs=specs(fwd, 0) + specs(bwd, 1),
        out_specs=[fwd(vw, 0), bwd(vw, 0)],
        out_shape=[o_shape, o_shape],
        scratch_shapes=[pltpu.VMEM((2 * (kw // LANES), LANES, 2 * LANES), F32),
                        pltpu.VMEM((4, CHUNK, LANES), F32)],
        compiler_params=pltpu.CompilerParams(dimension_semantics=("parallel", "arbitrary"),
                                             vmem_limit_bytes=VMEM_LIMIT),
        name="gla_scan",
    )(*([pack] * 3 + [la] + [pack] * 3 + [la]))


def _tail_kernel(x_ref, yf_ref, yb_ref, r_ref, k_ref, v_ref, g_ref, of_ref, ob_ref, zg_ref, p_ref,
                 seg_ref, rk_ref, lnw_ref, lnb_ref, gn_ref, wor_ref, wog_ref, nmix_ref,
                 nfpre_ref, nfpost_ref, wgate_ref, wup_ref, wdown_ref, wpp_ref, wpg_ref, bpg_ref, nple_ref,
                 out_ref):
    dot = functools.partial(jnp.dot, preferred_element_type=F32)
    tm = x_ref.shape[0]
    halves = [slice(0, tm // 2), slice(tm // 2, tm)]
    both = lambda fn, *lists: [fn(*args) for args in zip(*lists)]
    seg = seg_ref[...]
    inv_n = 1.0 / RWKV_HEAD_DIM
    y = [yf_ref[s, :] + yb_ref[s, :] for s in halves]
    mu = [_mm(a, seg) * inv_n for a in y]
    yc = both(lambda a, b: a - b, y, mu)
    var = [_mm(a * a, seg) * inv_n for a in yc]
    bonus_sum = [_mm(r_ref[s, :].astype(F32) * k_ref[s, :].astype(F32) * rk_ref[...], seg) for s in halves]
    y_r = [((c * lax.rsqrt(v + RWKV_GN_EPS) * lnw_ref[...] + lnb_ref[...] + b * v_ref[s, :].astype(F32))
            * g_ref[s, :].astype(F32)).astype(BF16) for c, v, b, s in zip(yc, var, bonus_sum, halves)]
    o = [of_ref[s, :] + ob_ref[s, :] for s in halves]
    gate_g = [zg_ref[s, :].astype(F32) for s in halves]
    y_g = [jnp.concatenate(
        [_rms(a[:, LANES * h:LANES * (h + 1)], gn_ref[...]) * _silu(b[:, LANES * h:LANES * (h + 1)])
         for h in range(GLA_HEADS)], axis=1).astype(BF16) for a, b in zip(o, gate_g)]
    mix = both(lambda a, b: dot(a, wor_ref[...]) + dot(b, wog_ref[...]), y_r, y_g)
    h1 = [x_ref[s, :] + _rms(a, nmix_ref[...]) for a, s in zip(mix, halves)]
    hn = [_rms(a, nfpre_ref[...]).astype(BF16) for a in h1]
    ffn_gate = [dot(a, wgate_ref[...]) for a in hn]
    ffn_up = [dot(a, wup_ref[...]) for a in hn]
    act = both(lambda a, b: (_silu(a) * b).astype(BF16), ffn_gate, ffn_up)
    down = [dot(a, wdown_ref[...]) for a in act]
    h2 = both(lambda a, b: a + _rms(b, nfpost_ref[...]), h1, down)
    e = [dot(p_ref[s, :].astype(BF16), wpp_ref[...]) for s in halves]
    gate = [_sigmoid(dot(a.astype(BF16), wpg_ref[...]) + bpg_ref[...]) for a in h2]
    for a, b, c, s in zip(h2, gate, e, halves):
        out_ref[s, :] = a + _rms(b * c, nple_ref[...])


def _tail(x2, y_f, y_b, pack, o_f, o_b, ggate, p2, consts):
    m, d = x2.shape
    tm = ROW_TILE
    w = y_f.shape[1]
    row = lambda width, j=0: pl.BlockSpec((tm, width), lambda i: (i, j))
    big = lambda arr: arr.size * arr.dtype.itemsize >= (1 << 20)
    return pl.pallas_call(
        _tail_kernel,
        grid=(m // tm,),
        in_specs=[row(d), row(w), row(w), row(w, 0), row(w, 1), row(w, 2), row(w, 5), row(w), row(w),
                  row(w), row(p2.shape[1])] + [_const_spec(c, 1, single_buffer=big(c)) for c in consts],
        out_specs=row(d),
        out_shape=jax.ShapeDtypeStruct((m, d), F32),
        compiler_params=pltpu.CompilerParams(dimension_semantics=("parallel",),
                                             vmem_limit_bytes=VMEM_LIMIT),
        name="tail",
    )(x2, y_f, y_b, pack, pack, pack, pack, o_f, o_b, ggate, p2, *consts)


def _pad_cols(w, n):
    return jnp.pad(w, ((0, 0), (0, n - w.shape[1])))


def _layer(h, p_i, norm_mix_pre, norm_mix_post, norm_ffn_pre, norm_ffn_post, norm_ple,
           w_in, rwkv_mu, rwkv_w0, rwkv_w_up, rwkv_a0, rwkv_a_up, rwkv_g_up,
           rwkv_k_k, rwkv_k_a, rwkv_r_k, rwkv_ln_w, rwkv_ln_b,
           gla_conv, gla_a_up, gla_a_b, gla_norm, w_out,
           ffn_gate, ffn_up, ffn_down, ple_proj, ple_gate, ple_gate_b):
    bsz, t_len, d = h.shape
    m = bsz * t_len
    rw_w = rwkv_w0.shape[1]
    kw = gla_a_b.shape[1]
    vw = w_out.shape[0] - rw_w
    assert vw == rw_w and t_len % (GLA_BLOCK_CHUNKS * CHUNK) == 0 and m % ROW_TILE == 0
    rwkv_in = 3 * rw_w + 2 * RWKV_DECAY_LORA + RWKV_AAA_LORA + RWKV_GATE_LORA
    row = lambda vec: vec.reshape(1, -1)

    lo_w = 3 * rw_w + 2 * RWKV_DECAY_LORA
    a_end = lo_w + RWKV_AAA_LORA

    def rwkv_cols(wmat):
        return jnp.concatenate([wmat[:, :lo_w], _pad_cols(wmat[:, lo_w:a_end], LANES),
                                wmat[:, a_end:rwkv_in]], axis=1)

    w_r = rwkv_cols(w_in[:, :rwkv_in]).astype(BF16)
    mu = rwkv_cols(row(rwkv_mu))
    gla_in = w_in.shape[1] - rwkv_in
    gla_main = gla_in - 2 * GLA_GATE_LORA
    w_g = jnp.concatenate([w_in[:, rwkv_in:rwkv_in + gla_main],
                           _pad_cols(w_in[:, rwkv_in + gla_main:], LANES)], axis=1).astype(BF16)

    zeros = jnp.zeros((RWKV_DECAY_LORA, rw_w), F32)
    w_up = jnp.concatenate([jnp.concatenate([rwkv_w_up[0], zeros], axis=1),
                            jnp.concatenate([zeros, rwkv_w_up[1]], axis=1)], axis=0).astype(BF16)
    a_up = jnp.pad(rwkv_a_up, ((0, LANES - RWKV_AAA_LORA), (0, 0))).astype(BF16)
    g_up = rwkv_g_up.astype(BF16)
    gz = jnp.zeros((GLA_GATE_LORA, kw), F32)
    gla_up = jnp.concatenate([jnp.concatenate([gla_a_up[0], gz], axis=1),
                              jnp.concatenate([gz, gla_a_up[1]], axis=1)], axis=0)
    gla_up = jnp.pad(gla_up, ((0, LANES - 2 * GLA_GATE_LORA), (0, 0))).astype(BF16)
    head_id = jnp.arange(rw_w) // RWKV_HEAD_DIM
    seg = (head_id[:, None] == head_id[None, :]).astype(BF16)

    x2 = h.reshape(m, d)
    rwkv_consts = (mu, rwkv_w0.reshape(1, -1), w_up, row(rwkv_a0), a_up, g_up, row(rwkv_k_k), row(rwkv_k_a),
                   seg) + _chunk_tri_tiles()
    gla_consts = (gla_conv.reshape(gla_conv.shape[0], -1), gla_up, gla_a_b.reshape(1, -1))
    pack, lw, gpack, la, ggate = _front(h, row(norm_mix_pre), w_r, w_g, rwkv_consts, gla_consts, rw_w, kw, vw)
    y_f, y_b = _rwkv_scan(pack, lw, rw_w, next(r for r in (RWKV_SCAN_ROWS, 2, 1) if bsz % r == 0))
    o_f, o_b = _gla_scan(gpack, la, kw, vw)

    w_o = w_out.astype(BF16)
    consts = (seg, row(rwkv_r_k), row(rwkv_ln_w), row(rwkv_ln_b), row(gla_norm), w_o[:rw_w], w_o[rw_w:],
              row(norm_mix_post), row(norm_ffn_pre), row(norm_ffn_post), ffn_gate.astype(BF16),
              ffn_up.astype(BF16), ffn_down.astype(BF16), ple_proj.astype(BF16), ple_gate.astype(BF16),
              row(ple_gate_b), row(norm_ple))
    out = _tail(x2, y_f.reshape(m, rw_w), y_b.reshape(m, rw_w), pack.reshape(m, -1), o_f.reshape(m, vw),
                o_b.reshape(m, vw), ggate.reshape(m, vw), p_i.reshape(m, -1), consts)
    return out.reshape(bsz, t_len, d)


def kernel(x, p, norm_mix_pre, norm_mix_post, norm_ffn_pre, norm_ffn_post, norm_ple, w_in, rwkv_mu, rwkv_w0, rwkv_w_up, rwkv_a0, rwkv_a_up, rwkv_g_up, rwkv_k_k, rwkv_k_a, rwkv_r_k, rwkv_ln_w, rwkv_ln_b, gla_conv, gla_a_up, gla_a_b, gla_norm, w_out, ffn_gate, ffn_up, ffn_down, ple_proj, ple_gate, ple_gate_b):
    params = (norm_mix_pre, norm_mix_post, norm_ffn_pre, norm_ffn_post, norm_ple, w_in, rwkv_mu,
              rwkv_w0, rwkv_w_up, rwkv_a0, rwkv_a_up, rwkv_g_up, rwkv_k_k, rwkv_k_a, rwkv_r_k,
              rwkv_ln_w, rwkv_ln_b, gla_conv, gla_a_up, gla_a_b, gla_norm, w_out, ffn_gate, ffn_up,
              ffn_down, ple_proj, ple_gate, ple_gate_b)
    h = x
    for i in range(p.shape[0]):
        h = _layer(h, p[i], *(w[i] for w in params))
    return h
```

```python
import functools

import jax
import jax.numpy as jnp
from jax import lax
from jax.experimental import pallas as pl
from jax.experimental.pallas import tpu as pltpu

F32 = jnp.float32
BF16 = jnp.bfloat16

NORM_EPS = 1e-6
RWKV_GN_EPS = 64e-5
RWKV_HEAD_DIM = 64
RWKV_DECAY_LORA = 64
RWKV_AAA_LORA = 64
RWKV_GATE_LORA = 128
GLA_HEADS = 4
GLA_GATE_LORA = 16
GLA_GATE_TEMP = 16.0
DECAY_SCALE = 0.6065306597126334
CHUNK = 64
RWKV_SCAN_ROWS = 4
GLA_BLOCK_CHUNKS = 4
GLA_SCAN_ROWS = 2
GLA_FACTORISED_LOG_RANGE = 60.0
LANES = 128
VMEM_LIMIT = 56 * 1024 * 1024
ROW_TILE = 256


def _mm(a, b):
    return jnp.dot(a.astype(BF16), b.astype(BF16), preferred_element_type=F32)


def _mm_nt(a, b):
    return lax.dot_general(a.astype(BF16), b.astype(BF16), (((1,), (1,)), ((), ())),
                           preferred_element_type=F32)


def _mm_tn(a, b):
    return lax.dot_general(a.astype(BF16), b.astype(BF16), (((0,), (0,)), ((), ())),
                           preferred_element_type=F32)


def _mm_split_lhs(a_bf16, x):
    hi = x.astype(BF16)
    lo = (x - hi.astype(F32)).astype(BF16)
    d = functools.partial(jnp.dot, preferred_element_type=F32)
    return d(a_bf16, hi) + d(a_bf16, lo)


def _rms(x, w):
    ms = jnp.mean(x * x, axis=-1, keepdims=True)
    return x * lax.rsqrt(ms + NORM_EPS) * w


def _softplus(u):
    return jnp.maximum(u, 0.0) + jnp.log(1.0 + jnp.exp(-jnp.abs(u)))


def _sigmoid(u):
    return 1.0 / (1.0 + jnp.exp(-u))


def _silu(u):
    return u * _sigmoid(u)


HALO = 8


def _halo_specs(tp, width, t_len):
    n_halo_blocks = t_len // HALO
    per_tile = tp // HALO
    main = pl.BlockSpec((1, tp, width), lambda b, t: (b, t, 0))
    prev = pl.BlockSpec((1, HALO, width), lambda b, t: (b, jnp.maximum(t * per_tile - 1, 0), 0))
    nxt = pl.BlockSpec((1, HALO, width), lambda b, t: (b, jnp.minimum((t + 1) * per_tile, n_halo_blocks - 1), 0))
    return [main, prev, nxt]


def _pair_stack(z, first_half):
    zero = jnp.zeros_like(z)
    return jnp.concatenate([jnp.where(first_half, z, zero), jnp.where(first_half, zero, z)], axis=0)


def _chunk_causal(reverse):
    r = lax.broadcasted_iota(jnp.int32, (CHUNK, CHUNK), 0)
    c = lax.broadcasted_iota(jnp.int32, (CHUNK, CHUNK), 1)
    return (c >= r) if reverse else (c <= r)


def _const_spec(arr, grid_rank, single_buffer=False):
    idx = (lambda i: (0,) * arr.ndim) if grid_rank == 1 else (lambda b, t: (0,) * arr.ndim)
    if single_buffer:
        return pl.BlockSpec(arr.shape, idx, pipeline_mode=pl.Buffered(1))
    return pl.BlockSpec(arr.shape, idx)


def _front_kernel(x_ref, xp_ref, xn_ref, nw_ref, wr_ref, wg_ref,
                  mu_ref, w0_ref, wup_ref, a0_ref, aup_ref, gup_ref, kk_ref, ka_ref, seg_ref, trif_ref, trib_ref,
                  cw_ref, gaup_ref, gab_ref,
                  pack_ref, g_ref, gpack_ref, la_ref, ggate_ref, *, width, kw, vw):
    t = pl.program_id(1)
    nt = pl.num_programs(1)
    tp = x_ref.shape[1]
    rows = tp + 2 * HALO
    xa = jnp.concatenate([xp_ref[0], x_ref[0], xn_ref[0]], axis=0)
    ridx = lax.broadcasted_iota(jnp.int32, (rows, 1), 0)
    inside = ((ridx >= HALO) | (t > 0)) & ((ridx < HALO + tp) | (t < nt - 1))
    xn = jnp.where(inside, _rms(xa, nw_ref[...]), 0.0).astype(BF16)
    centre = slice(HALO, HALO + tp)

    def with_neighbours(z):
        return z[centre], pltpu.roll(z, 1, 0)[centre], pltpu.roll(z, rows - 1, 0)[centre]

    z_r = jnp.dot(xn, wr_ref[...], preferred_element_type=F32)
    z_g = jnp.dot(xn, wg_ref[...], preferred_element_type=F32)
    _rwkv_prep_body(*with_neighbours(z_r), mu_ref, w0_ref, wup_ref, a0_ref, aup_ref, gup_ref, kk_ref, ka_ref,
                    seg_ref, trif_ref, trib_ref, pack_ref, g_ref, width=width)
    qkv_w = 2 * kw + vw
    ggate_ref[0] = z_g[centre, qkv_w:qkv_w + vw].astype(ggate_ref.dtype)
    _gla_prep_body(*with_neighbours(z_g[:, :qkv_w]), z_g[centre, qkv_w + vw:qkv_w + vw + LANES],
                   cw_ref, gaup_ref, gab_ref, gpack_ref, la_ref, kw=kw)


def _front(x, nw, w_r, w_g, rwkv_consts, gla_consts, width, kw, vw):
    bsz, t_len, d = x.shape
    tp = ROW_TILE
    consts = (nw, w_r, w_g) + tuple(rwkv_consts) + tuple(gla_consts)
    big = lambda arr: arr.size * arr.dtype.itemsize >= (1 << 20)
    out = lambda w: pl.BlockSpec((1, tp, w), lambda b, t: (b, t, 0))
    shape = lambda w, dt: jax.ShapeDtypeStruct((bsz, t_len, w), dt)
    return pl.pallas_call(
        functools.partial(_front_kernel, width=width, kw=kw, vw=vw),
        grid=(bsz, t_len // tp),
        in_specs=_halo_specs(tp, d, t_len) + [_const_spec(c, 2, single_buffer=big(c)) for c in consts],
        out_specs=[out(6 * width), out(2 * width), out(2 * kw + vw), out(2 * kw), out(vw)],
        out_shape=[shape(6 * width, BF16), shape(2 * width, F32), shape(2 * kw + vw, BF16),
                   shape(2 * kw, F32), shape(vw, BF16)],
        compiler_params=pltpu.CompilerParams(dimension_semantics=("parallel", "parallel"),
                                             vmem_limit_bytes=VMEM_LIMIT),
        name="front",
    )(x, x, x, *consts)


def _rwkv_prep_body(z, z_dn, z_up, mu_ref, w0_ref, wup_ref, a0_ref, aup_ref, gup_ref,
                    kk_ref, ka_ref, seg_ref, trif_ref, trib_ref, pack_ref, g_ref, *, width):
    w = width
    zs = z + mu_ref[...] * (0.5 * (z_dn + z_up) - z)
    r = zs[:, 0:w]
    k = zs[:, w:2 * w]
    v = zs[:, 2 * w:3 * w]
    w_lo = zs[:, 3 * w:3 * w + LANES]
    a_lo = zs[:, 3 * w + LANES:3 * w + 2 * LANES]
    g_lo = zs[:, 3 * w + 2 * LANES:3 * w + 3 * LANES]
    lw = -DECAY_SCALE * _sigmoid(w0_ref[...] + _mm(jnp.tanh(w_lo), wup_ref[...]))
    g_ref[0, :, 0:w] = _mm_split_lhs(trif_ref[...], lw[:, 0:w])
    g_ref[0, :, w:2 * w] = _mm_split_lhs(trib_ref[...], lw[:, w:2 * w])
    a = _sigmoid(a0_ref[...] + _mm(a_lo, aup_ref[...]))
    g = _mm(_sigmoid(g_lo), gup_ref[...])
    kk = k * kk_ref[...]
    ss = _mm(kk * kk, seg_ref[...])
    kk = kk * lax.rsqrt(jnp.maximum(ss, 1e-24))
    k = k * (1.0 + (a - 1.0) * ka_ref[...])
    for j, val in enumerate((r, k, v, kk, kk * a, g)):
        pack_ref[0, :, j * w:(j + 1) * w] = val.astype(pack_ref.dtype)


def _chunk_tri_tiles():
    chunk_id = jnp.arange(ROW_TILE) // CHUNK
    pos = jnp.arange(ROW_TILE)
    same = chunk_id[:, None] == chunk_id[None, :]
    tri_f = (same & (pos[None, :] <= pos[:, None])).astype(BF16)
    tri_b = (same & (pos[None, :] >= pos[:, None])).astype(BF16)
    return tri_f, tri_b


def _delta_chunk_problems(refs, row, reverse, out_ref, slot0):
    r_ref, k_ref, v_ref, kk_ref, b_ref, g_ref = refs
    c = CHUNK
    g = g_ref[row]
    trow = lax.broadcasted_iota(jnp.int32, (c, 1), 0)
    if reverse:
        total = g[0:1, :]
        g_excl = jnp.where(trow == c - 1, 0.0, pltpu.roll(g, c - 1, 0))
    else:
        total = g[c - 1:c, :]
        g_excl = jnp.where(trow == 0, 0.0, pltpu.roll(g, 1, 0))
    einv = jnp.exp(-g)
    ec = jnp.exp(total - g)
    kk = kk_ref[row].astype(F32)
    bb = b_ref[row].astype(F32)
    kx = k_ref[row].astype(F32)
    full = dict(a=(-kk * jnp.exp(g_excl)).astype(BF16), r=(r_ref[row].astype(F32) * jnp.exp(g)).astype(BF16),
                bt=(bb * einv).astype(BF16), kt=(kx * einv).astype(BF16), v=v_ref[row],
                bh=(bb * ec).astype(BF16), kh=(kx * ec).astype(BF16))
    head0 = lax.broadcasted_iota(jnp.int32, (c, LANES), 1) < RWKV_HEAD_DIM
    tt = lax.broadcasted_iota(jnp.int32, (c, LANES), 0)
    ss = lax.broadcasted_iota(jnp.int32, (c, LANES), 1) % c
    incl = (ss >= tt) if reverse else (ss <= tt)
    strict = (ss > tt) if reverse else (ss < tt)
    probs = []
    for p in range(total.shape[1] // LANES):
        sl = slice(LANES * p, LANES * (p + 1))
        q = {name: z[:, sl] for name, z in full.items()}
        gam_col = jnp.transpose(jnp.broadcast_to(jnp.exp(total[:, sl]), (LANES, LANES)))
        q.update(head0=head0, incl=incl, strict=strict, gam_col=gam_col, lanes=sl, row=row, out=out_ref,
                 slot=slot0 + p)
        probs.append(q)
    return probs


def _rwkv_scan_kernel(*refs, rows):
    in_f, in_b = refs[0:6], refs[6:12]
    yf_ref, yb_ref, h_ref = refs[12:15]

    @pl.when(pl.program_id(1) == 0)
    def _():
        h_ref[...] = jnp.zeros_like(h_ref)

    c = CHUNK
    n = LANES
    n_pairs = yf_ref.shape[-1] // LANES
    probs = []
    for row in range(rows):
        probs += _delta_chunk_problems(in_f, row, False, yf_ref, (2 * row) * n_pairs)
        probs += _delta_chunk_problems(in_b, row, True, yb_ref, (2 * row + 1) * n_pairs)
    stack = lambda z, q: _pair_stack(z.astype(BF16), q["head0"])
    rr = lax.broadcasted_iota(jnp.int32, (n, n), 0) // c
    cc = lax.broadcasted_iota(jnp.int32, (n, n), 1) // c
    same_head = rr == cc
    zero = jnp.zeros((c, n), F32)
    ar = [jnp.concatenate([q["a"], q["r"]], axis=0) for q in probs]
    gm = [_mm_nt(x, jnp.concatenate([stack(q["bt"], q), stack(q["kt"], q)], axis=0)) for x, q in zip(ar, probs)]
    hs = [h_ref[q["slot"]] for q in probs]
    hb = [h.astype(BF16) for h in hs]
    rh = [_mm(q["r"], h) for q, h in zip(probs, hb)]
    a_ab = [jnp.where(q["strict"], m[:c, :n], zero) for q, m in zip(probs, gm)]
    a_ak = [jnp.where(q["strict"], m[:c, n:], zero) for q, m in zip(probs, gm)]
    a_rbk = [jnp.where(jnp.concatenate([q["incl"], q["incl"]], axis=1), m[c:], 0.0) for q, m in zip(probs, gm)]
    vs = [stack(q["v"], q) for q in probs]
    zs = [_mm(jnp.concatenate([q["a"], a.astype(BF16)], axis=1), jnp.concatenate([h, v], axis=0))
          for q, a, h, v in zip(probs, a_ak, hb, vs)]
    ps = a_ab
    levels = CHUNK.bit_length() - 1
    for level in range(levels - 1):
        prods = [_mm(p, jnp.concatenate([stack(z, q), stack(p, q)], axis=1)) for p, z, q in zip(ps, zs, probs)]
        zs = [z + pr[:, :n] for z, pr in zip(zs, prods)]
        ps = [pr[:, n:] for pr in prods]
    us = [z + _mm(p, stack(z, q)) for p, z, q in zip(ps, zs, probs)]
    ys = [x + _mm(a, jnp.concatenate([stack(u, q), v], axis=0))
          for x, a, u, v, q in zip(rh, a_rbk, us, vs, probs)]
    upd = [_mm_tn(jnp.concatenate([q["bh"], q["kh"]], axis=0), jnp.concatenate([u.astype(BF16), q["v"]], axis=0))
           for q, u in zip(probs, us)]
    for q, y, h, d in zip(probs, ys, hs, upd):
        q["out"][q["row"], :, q["lanes"]] = y
        h_ref[q["slot"]] = h * q["gam_col"] + jnp.where(same_head, d, 0.0)


def _rwkv_scan(pack, g, width, rows):
    bsz, t_len, _ = pack.shape
    nc = t_len // CHUNK
    fwd = lambda j: pl.BlockSpec((rows, CHUNK, width), lambda b, t: (b, t, j))
    bwd = lambda j: pl.BlockSpec((rows, CHUNK, width), lambda b, t: (b, nc - 1 - t, j))
    n_pairs = width // LANES
    y_shape = jax.ShapeDtypeStruct((bsz, t_len, width), F32)
    return pl.pallas_call(
        functools.partial(_rwkv_scan_kernel, rows=rows),
        grid=(bsz // rows, nc),
        in_specs=[fwd(j) for j in (0, 1, 2, 3, 4)] + [fwd(0)] + [bwd(j) for j in (0, 1, 2, 3, 4)] + [bwd(1)],
        out_specs=[fwd(0), bwd(0)],
        out_shape=[y_shape, y_shape],
        scratch_shapes=[pltpu.VMEM((2 * rows * n_pairs, LANES, LANES), F32)],
        compiler_params=pltpu.CompilerParams(dimension_semantics=("parallel", "arbitrary"),
                                             vmem_limit_bytes=VMEM_LIMIT),
        name="rwkv_scan",
    )(*([pack] * 5 + [g] + [pack] * 5 + [g]))


def _gla_prep_body(x, x_dn, x_up, a_lo, cw_ref, aup_ref, ab_ref, pack_ref, la_ref, *, kw):
    qkv_w = x.shape[1]
    qkv = _silu(cw_ref[0:1, :] * x_dn + cw_ref[1:2, :] * x + cw_ref[2:3, :] * x_up)
    logit = _mm(a_lo, aup_ref[...]) + ab_ref[...]
    la_ref[0] = -_softplus(-logit) * (1.0 / GLA_GATE_TEMP)
    dk = kw // GLA_HEADS
    pack_ref[0, :, 0:kw] = (qkv[:, 0:kw] * (dk ** -0.5)).astype(pack_ref.dtype)
    pack_ref[0, :, kw:qkv_w] = qkv[:, kw:qkv_w].astype(pack_ref.dtype)


def _gla_block_problems(refs, seq, reverse, slot0, out_ref):
    q_ref, k_ref, v_ref, la_ref = refs
    c = CHUNK
    nb = la_ref.shape[1] // c
    tri_bf = _chunk_causal(reverse).astype(BF16)
    tt = lax.broadcasted_iota(jnp.int32, (c, LANES), 0)
    ss = lax.broadcasted_iota(jnp.int32, (c, LANES), 1) % c
    causal = (ss >= tt) if reverse else (ss <= tt)
    probs = []
    order = range(nb - 1, -1, -1) if reverse else range(nb)
    for step, ci in enumerate(order):
        rows = slice(ci * c, (ci + 1) * c)
        la = la_ref[seq, rows, :]
        bc = _mm_split_lhs(tri_bf, la)
        total = bc[0:1, :] if reverse else bc[c - 1:c, :]
        kx = k_ref[seq, rows, :].astype(F32)
        q_t = (q_ref[seq, rows, :].astype(F32) * jnp.exp(bc)).astype(BF16)
        k_t = (kx * jnp.exp(-bc)).astype(BF16)
        k_h = (kx * jnp.exp(total - bc)).astype(BF16)
        lowest = jnp.min(total)
        for p in range(la.shape[1] // LANES):
            sl = slice(LANES * p, LANES * (p + 1))
            vsl = slice(2 * LANES * p, 2 * LANES * (p + 1))
            gam = jnp.transpose(jnp.broadcast_to(jnp.exp(total[:, sl]), (LANES, LANES)))
            probs.append(dict(q=q_t[:, sl], kt=k_t[:, sl], kh=k_h[:, sl], v=v_ref[seq, rows, vsl], causal=causal,
                              gam_col=jnp.concatenate([gam, gam], axis=1), slot=slot0 + p, step=step, rows=rows,
                              lanes=vsl, out=out_ref, reverse=reverse, lowest=lowest, bc=bc[:, sl],
                              k_raw=kx[:, sl], q_ref=q_ref, key_lanes=sl, seq=seq))
    return probs


def _gla_intra_pairwise(q, inter, rob_ref):
    c = CHUNK
    bc = q["bc"]
    q_raw = q["q_ref"][q["seq"], q["rows"], q["key_lanes"]].astype(F32)
    rob_ref[0] = bc
    rob_ref[1] = q["k_raw"]
    rob_ref[2] = q["v"][:, :LANES].astype(F32)
    rob_ref[3] = q["v"][:, LANES:].astype(F32)
    trow = lax.broadcasted_iota(jnp.int32, (c, 1), 0)
    head0 = lax.broadcasted_iota(jnp.int32, (c, LANES), 1) < (LANES // 2)

    def body(j, acc):
        acc0, acc1 = acc
        b_j = rob_ref[0, pl.ds(j, 1), :]
        k_j = rob_ref[1, pl.ds(j, 1), :]
        visible = (trow <= j) if q["reverse"] else (trow >= j)
        pj = q_raw * k_j * jnp.exp(jnp.where(visible, bc - b_j, -jnp.inf))
        s0 = jnp.sum(jnp.where(head0, pj, 0.0), axis=1, keepdims=True)
        s1 = jnp.sum(jnp.where(head0, 0.0, pj), axis=1, keepdims=True)
        return acc0 + s0 * rob_ref[2, pl.ds(j, 1), :], acc1 + s1 * rob_ref[3, pl.ds(j, 1), :]

    zero = jnp.zeros((c, LANES), F32)
    acc0, acc1 = lax.fori_loop(0, c, body, (zero, zero))
    q["out"][q["seq"], q["rows"], q["lanes"]] = inter + jnp.concatenate([acc0, acc1], axis=1)


def _gla_scan_kernel(*refs):
    in_f, in_b = refs[0:4], refs[4:8]
    of_ref, ob_ref, st_ref, rob_ref = refs[8:12]

    @pl.when(pl.program_id(1) == 0)
    def _():
        st_ref[...] = jnp.zeros_like(st_ref)

    c = CHUNK
    n_slots = st_ref.shape[0]
    n_seq = of_ref.shape[0]
    n_pairs = n_slots // (2 * n_seq)
    probs = []
    for seq in range(n_seq):
        probs += _gla_block_problems(in_f, seq, False, (2 * seq) * n_pairs, of_ref)
        probs += _gla_block_problems(in_b, seq, True, (2 * seq + 1) * n_pairs, ob_ref)
    n_steps = 1 + max(q["step"] for q in probs)
    head0_k = lax.broadcasted_iota(jnp.int32, (c, LANES), 1) < (LANES // 2)
    head0_v = lax.broadcasted_iota(jnp.int32, (c, 2 * LANES), 1) < LANES
    same_head = (lax.broadcasted_iota(jnp.int32, (LANES, 2 * LANES), 0) // (LANES // 2)
                 == lax.broadcasted_iota(jnp.int32, (LANES, 2 * LANES), 1) // LANES)
    scores = [jnp.where(q["causal"], _mm_nt(q["q"], _pair_stack(q["kt"], head0_k)), 0.0) for q in probs]
    upd = [jnp.where(same_head, _mm_tn(q["kh"], q["v"]), 0.0) for q in probs]
    state = [st_ref[s] for s in range(n_slots)]
    entering = [None] * len(probs)
    for step in range(n_steps):
        for i, q in enumerate(probs):
            if q["step"] == step:
                entering[i] = state[q["slot"]]
                state[q["slot"]] = state[q["slot"]] * q["gam_col"] + upd[i]
    outs = [_mm(jnp.concatenate([s.astype(BF16), q["q"]], axis=1),
                jnp.concatenate([_pair_stack(q["v"], head0_v), st.astype(BF16)], axis=0))
            for s, q, st in zip(scores, probs, entering)]
    for q, o in zip(probs, outs):
        q["out"][q["seq"], q["rows"], q["lanes"]] = o
    for s in range(n_slots):
        st_ref[s] = state[s]

    lowest = functools.reduce(jnp.minimum, [q["lowest"] for q in probs])

    @pl.when(lowest < -GLA_FACTORISED_LOG_RANGE)
    def _():
        for q, st in zip(probs, entering):
            _gla_intra_pairwise(q, _mm(q["q"], st), rob_ref)


def _gla_scan(pack, la, kw, vw, n_seq):
    bsz, t_len, _ = pack.shape
    rows = GLA_BLOCK_CHUNKS * CHUNK
    nb = t_len // rows
    fwd = lambda w, j: pl.BlockSpec((n_seq, rows, w), lambda b, t: (b, t, j))
    bwd = lambda w, j: pl.BlockSpec((n_seq, rows, w), lambda b, t: (b, nb - 1 - t, j))
    o_shape = jax.ShapeDtypeStruct((bsz, t_len, vw), F32)
    specs = lambda blk, la_col: [blk(kw, 0), blk(kw, 1), blk(vw, (2 * kw) // vw), blk(kw, la_col)]
    return pl.pallas_call(
        _gla_scan_kernel,
        grid=(bsz // n_seq, nb),
        in_specs=specs(fwd, 0) + specs(bwd, 1),
        out_specs=[fwd(vw, 0), bwd(vw, 0)],
        out_shape=[o_shape, o_shape],
        scratch_shapes=[pltpu.VMEM((2 * n_seq * (kw // LANES), LANES, 2 * LANES), F32),
                        pltpu.VMEM((4, CHUNK, LANES), F32)],
        compiler_params=pltpu.CompilerParams(dimension_semantics=("parallel", "arbitrary"),
                                             vmem_limit_bytes=VMEM_LIMIT),
        name="gla_scan",
    )(*([pack] * 3 + [la] + [pack] * 3 + [la]))


def _tail_kernel(x_ref, yf_ref, yb_ref, r_ref, k_ref, v_ref, g_ref, of_ref, ob_ref, zg_ref, p_ref,
                 seg_ref, rk_ref, lnw_ref, lnb_ref, gn_ref, wor_ref, wog_ref, nmix_ref,
                 nfpre_ref, nfpost_ref, wgate_ref, wup_ref, wdown_ref, wpp_ref, wpg_ref, bpg_ref, nple_ref,
                 out_ref):
    dot = functools.partial(jnp.dot, preferred_element_type=F32)
    tm = x_ref.shape[0]
    halves = [slice(0, tm // 2), slice(tm // 2, tm)]
    both = lambda fn, *lists: [fn(*args) for args in zip(*lists)]
    seg = seg_ref[...]
    inv_n = 1.0 / RWKV_HEAD_DIM
    y = [yf_ref[s, :] + yb_ref[s, :] for s in halves]
    mu = [_mm(a, seg) * inv_n for a in y]
    yc = both(lambda a, b: a - b, y, mu)
    var = [_mm(a * a, seg) * inv_n for a in yc]
    bonus_sum = [_mm(r_ref[s, :].astype(F32) * k_ref[s, :].astype(F32) * rk_ref[...], seg) for s in halves]
    y_r = [((c * lax.rsqrt(v + RWKV_GN_EPS) * lnw_ref[...] + lnb_ref[...] + b * v_ref[s, :].astype(F32))
            * g_ref[s, :].astype(F32)).astype(BF16) for c, v, b, s in zip(yc, var, bonus_sum, halves)]
    o = [of_ref[s, :] + ob_ref[s, :] for s in halves]
    gate_g = [zg_ref[s, :].astype(F32) for s in halves]
    y_g = [jnp.concatenate(
        [_rms(a[:, LANES * h:LANES * (h + 1)], gn_ref[...]) * _silu(b[:, LANES * h:LANES * (h + 1)])
         for h in range(GLA_HEADS)], axis=1).astype(BF16) for a, b in zip(o, gate_g)]
    mix = both(lambda a, b: dot(a, wor_ref[...]) + dot(b, wog_ref[...]), y_r, y_g)
    h1 = [x_ref[s, :] + _rms(a, nmix_ref[...]) for a, s in zip(mix, halves)]
    hn = [_rms(a, nfpre_ref[...]).astype(BF16) for a in h1]
    ffn_gate = [dot(a, wgate_ref[...]) for a in hn]
    ffn_up = [dot(a, wup_ref[...]) for a in hn]
    act = both(lambda a, b: (_silu(a) * b).astype(BF16), ffn_gate, ffn_up)
    down = [dot(a, wdown_ref[...]) for a in act]
    h2 = both(lambda a, b: a + _rms(b, nfpost_ref[...]), h1, down)
    e = [dot(p_ref[s, :].astype(BF16), wpp_ref[...]) for s in halves]
    gate = [_sigmoid(dot(a.astype(BF16), wpg_ref[...]) + bpg_ref[...]) for a in h2]
    for a, b, c, s in zip(h2, gate, e, halves):
        out_ref[s, :] = a + _rms(b * c, nple_ref[...])


def _tail(x2, y_f, y_b, pack, o_f, o_b, ggate, p2, consts):
    m, d = x2.shape
    tm = ROW_TILE
    w = y_f.shape[1]
    row = lambda width, j=0: pl.BlockSpec((tm, width), lambda i: (i, j))
    big = lambda arr: arr.size * arr.dtype.itemsize >= (1 << 20)
    return pl.pallas_call(
        _tail_kernel,
        grid=(m // tm,),
        in_specs=[row(d), row(w), row(w), row(w, 0), row(w, 1), row(w, 2), row(w, 5), row(w), row(w),
                  row(w), row(p2.shape[1])] + [_const_spec(c, 1, single_buffer=big(c)) for c in consts],
        out_specs=row(d),
        out_shape=jax.ShapeDtypeStruct((m, d), F32),
        compiler_params=pltpu.CompilerParams(dimension_semantics=("parallel",),
                                             vmem_limit_bytes=VMEM_LIMIT),
        name="tail",
    )(x2, y_f, y_b, pack, pack, pack, pack, o_f, o_b, ggate, p2, *consts)


def _pad_cols(w, n):
    return jnp.pad(w, ((0, 0), (0, n - w.shape[1])))


def _layer(h, p_i, norm_mix_pre, norm_mix_post, norm_ffn_pre, norm_ffn_post, norm_ple,
           w_in, rwkv_mu, rwkv_w0, rwkv_w_up, rwkv_a0, rwkv_a_up, rwkv_g_up,
           rwkv_k_k, rwkv_k_a, rwkv_r_k, rwkv_ln_w, rwkv_ln_b,
           gla_conv, gla_a_up, gla_a_b, gla_norm, w_out,
           ffn_gate, ffn_up, ffn_down, ple_proj, ple_gate, ple_gate_b):
    bsz, t_len, d = h.shape
    m = bsz * t_len
    rw_w = rwkv_w0.shape[1]
    kw = gla_a_b.shape[1]
    vw = w_out.shape[0] - rw_w
    assert vw == rw_w and t_len % (GLA_BLOCK_CHUNKS * CHUNK) == 0 and m % ROW_TILE == 0
    rwkv_in = 3 * rw_w + 2 * RWKV_DECAY_LORA + RWKV_AAA_LORA + RWKV_GATE_LORA
    row = lambda vec: vec.reshape(1, -1)

    lo_w = 3 * rw_w + 2 * RWKV_DECAY_LORA
    a_end = lo_w + RWKV_AAA_LORA

    def rwkv_cols(wmat):
        return jnp.concatenate([wmat[:, :lo_w], _pad_cols(wmat[:, lo_w:a_end], LANES),
                                wmat[:, a_end:rwkv_in]], axis=1)

    w_r = rwkv_cols(w_in[:, :rwkv_in]).astype(BF16)
    mu = rwkv_cols(row(rwkv_mu))
    gla_in = w_in.shape[1] - rwkv_in
    gla_main = gla_in - 2 * GLA_GATE_LORA
    w_g = jnp.concatenate([w_in[:, rwkv_in:rwkv_in + gla_main],
                           _pad_cols(w_in[:, rwkv_in + gla_main:], LANES)], axis=1).astype(BF16)

    zeros = jnp.zeros((RWKV_DECAY_LORA, rw_w), F32)
    w_up = jnp.concatenate([jnp.concatenate([rwkv_w_up[0], zeros], axis=1),
                            jnp.concatenate([zeros, rwkv_w_up[1]], axis=1)], axis=0).astype(BF16)
    a_up = jnp.pad(rwkv_a_up, ((0, LANES - RWKV_AAA_LORA), (0, 0))).astype(BF16)
    g_up = rwkv_g_up.astype(BF16)
    gz = jnp.zeros((GLA_GATE_LORA, kw), F32)
    gla_up = jnp.concatenate([jnp.concatenate([gla_a_up[0], gz], axis=1),
                              jnp.concatenate([gz, gla_a_up[1]], axis=1)], axis=0)
    gla_up = jnp.pad(gla_up, ((0, LANES - 2 * GLA_GATE_LORA), (0, 0))).astype(BF16)
    head_id = jnp.arange(rw_w) // RWKV_HEAD_DIM
    seg = (head_id[:, None] == head_id[None, :]).astype(BF16)

    x2 = h.reshape(m, d)
    rwkv_consts = (mu, rwkv_w0.reshape(1, -1), w_up, row(rwkv_a0), a_up, g_up, row(rwkv_k_k), row(rwkv_k_a),
                   seg) + _chunk_tri_tiles()
    gla_consts = (gla_conv.reshape(gla_conv.shape[0], -1), gla_up, gla_a_b.reshape(1, -1))
    pack, lw, gpack, la, ggate = _front(h, row(norm_mix_pre), w_r, w_g, rwkv_consts, gla_consts, rw_w, kw, vw)
    y_f, y_b = _rwkv_scan(pack, lw, rw_w, next(r for r in (RWKV_SCAN_ROWS, 2, 1) if bsz % r == 0))
    o_f, o_b = _gla_scan(gpack, la, kw, vw, next(r for r in (GLA_SCAN_ROWS, 1) if bsz % r == 0))

    w_o = w_out.astype(BF16)
    consts = (seg, row(rwkv_r_k), row(rwkv_ln_w), row(rwkv_ln_b), row(gla_norm), w_o[:rw_w], w_o[rw_w:],
              row(norm_mix_post), row(norm_ffn_pre), row(norm_ffn_post), ffn_gate.astype(BF16),
              ffn_up.astype(BF16), ffn_down.astype(BF16), ple_proj.astype(BF16), ple_gate.astype(BF16),
              row(ple_gate_b), row(norm_ple))
    out = _tail(x2, y_f.reshape(m, rw_w), y_b.reshape(m, rw_w), pack.reshape(m, -1), o_f.reshape(m, vw),
                o_b.reshape(m, vw), ggate.reshape(m, vw), p_i.reshape(m, -1), consts)
    return out.reshape(bsz, t_len, d)


def kernel(x, p, norm_mix_pre, norm_mix_post, norm_ffn_pre, norm_ffn_post, norm_ple, w_in, rwkv_mu, rwkv_w0, rwkv_w_up, rwkv_a0, rwkv_a_up, rwkv_g_up, rwkv_k_k, rwkv_k_a, rwkv_r_k, rwkv_ln_w, rwkv_ln_b, gla_conv, gla_a_up, gla_a_b, gla_norm, w_out, ffn_gate, ffn_up, ffn_down, ple_proj, ple_gate, ple_gate_b):
    params = (norm_mix_pre, norm_mix_post, norm_ffn_pre, norm_ffn_post, norm_ple, w_in, rwkv_mu,
              rwkv_w0, rwkv_w_up, rwkv_a0, rwkv_a_up, rwkv_g_up, rwkv_k_k, rwkv_k_a, rwkv_r_k,
              rwkv_ln_w, rwkv_ln_b, gla_conv, gla_a_up, gla_a_b, gla_norm, w_out, ffn_gate, ffn_up,
              ffn_down, ple_proj, ple_gate, ple_gate_b)
    h = x
    for i in range(p.shape[0]):
        h = _layer(h, p[i], *(w[i] for w in params))
    return h
```

```python
import functools

import jax
import jax.numpy as jnp
from jax import lax
from jax.experimental import pallas as pl
from jax.experimental.pallas import tpu as pltpu

F32 = jnp.float32
BF16 = jnp.bfloat16

NORM_EPS = 1e-6
RWKV_GN_EPS = 64e-5
RWKV_HEAD_DIM = 64
RWKV_DECAY_LORA = 64
RWKV_AAA_LORA = 64
RWKV_GATE_LORA = 128
GLA_HEADS = 4
GLA_GATE_LORA = 16
GLA_GATE_TEMP = 16.0
DECAY_SCALE = 0.6065306597126334
CHUNK = 64
RWKV_SCAN_ROWS = 4
GLA_BLOCK_CHUNKS = 4
GLA_SCAN_ROWS = 2
GLA_FACTORISED_LOG_RANGE = 60.0
LANES = 128
VMEM_LIMIT = 56 * 1024 * 1024
ROW_TILE = 256


def _mm(a, b):
    return jnp.dot(a.astype(BF16), b.astype(BF16), preferred_element_type=F32)


def _mm_nt(a, b):
    return lax.dot_general(a.astype(BF16), b.astype(BF16), (((1,), (1,)), ((), ())),
                           preferred_element_type=F32)


def _mm_tn(a, b):
    return lax.dot_general(a.astype(BF16), b.astype(BF16), (((0,), (0,)), ((), ())),
                           preferred_element_type=F32)


def _mm_split_lhs(a_bf16, x):
    hi = x.astype(BF16)
    lo = (x - hi.astype(F32)).astype(BF16)
    d = functools.partial(jnp.dot, preferred_element_type=F32)
    return d(a_bf16, hi) + d(a_bf16, lo)


def _rms(x, w):
    ms = jnp.mean(x * x, axis=-1, keepdims=True)
    return x * lax.rsqrt(ms + NORM_EPS) * w


def _softplus(u):
    return jnp.maximum(u, 0.0) + jnp.log(1.0 + jnp.exp(-jnp.abs(u)))


def _sigmoid(u):
    return 1.0 / (1.0 + jnp.exp(-u))


def _silu(u):
    return u * _sigmoid(u)


HALO = 8


def _halo_specs(tp, width, t_len):
    n_halo_blocks = t_len // HALO
    per_tile = tp // HALO
    main = pl.BlockSpec((1, tp, width), lambda b, t: (b, t, 0))
    prev = pl.BlockSpec((1, HALO, width), lambda b, t: (b, jnp.maximum(t * per_tile - 1, 0), 0))
    nxt = pl.BlockSpec((1, HALO, width), lambda b, t: (b, jnp.minimum((t + 1) * per_tile, n_halo_blocks - 1), 0))
    return [main, prev, nxt]


def _pair_stack(z, first_half):
    zero = jnp.zeros_like(z)
    return jnp.concatenate([jnp.where(first_half, z, zero), jnp.where(first_half, zero, z)], axis=0)


def _chunk_causal(reverse):
    r = lax.broadcasted_iota(jnp.int32, (CHUNK, CHUNK), 0)
    c = lax.broadcasted_iota(jnp.int32, (CHUNK, CHUNK), 1)
    return (c >= r) if reverse else (c <= r)


def _const_spec(arr, grid_rank, single_buffer=False):
    idx = (lambda i: (0,) * arr.ndim) if grid_rank == 1 else (lambda b, t: (0,) * arr.ndim)
    if single_buffer:
        return pl.BlockSpec(arr.shape, idx, pipeline_mode=pl.Buffered(1))
    return pl.BlockSpec(arr.shape, idx)


def _front_kernel(x_ref, xp_ref, xn_ref, nw_ref, wr_ref, wg_ref,
                  mu_ref, w0_ref, wup_ref, a0_ref, aup_ref, gup_ref, kk_ref, ka_ref, seg_ref, trif_ref, trib_ref,
                  cw_ref, gaup_ref, gab_ref,
                  pack_ref, g_ref, gpack_ref, la_ref, ggate_ref, *, width, kw, vw):
    t = pl.program_id(1)
    nt = pl.num_programs(1)
    tp = x_ref.shape[1]
    rows = tp + 2 * HALO
    xa = jnp.concatenate([xp_ref[0], x_ref[0], xn_ref[0]], axis=0)
    ridx = lax.broadcasted_iota(jnp.int32, (rows, 1), 0)
    inside = ((ridx >= HALO) | (t > 0)) & ((ridx < HALO + tp) | (t < nt - 1))
    xn = jnp.where(inside, _rms(xa, nw_ref[...]), 0.0).astype(BF16)
    centre = slice(HALO, HALO + tp)

    def with_neighbours(z):
        return z[centre], pltpu.roll(z, 1, 0)[centre], pltpu.roll(z, rows - 1, 0)[centre]

    z_r = jnp.dot(xn, wr_ref[...], preferred_element_type=F32)
    z_g = jnp.dot(xn, wg_ref[...], preferred_element_type=F32)
    _rwkv_prep_body(*with_neighbours(z_r), mu_ref, w0_ref, wup_ref, a0_ref, aup_ref, gup_ref, kk_ref, ka_ref,
                    seg_ref, trif_ref, trib_ref, pack_ref, g_ref, width=width)
    qkv_w = 2 * kw + vw
    ggate_ref[0] = z_g[centre, qkv_w:qkv_w + vw].astype(ggate_ref.dtype)
    _gla_prep_body(*with_neighbours(z_g[:, :qkv_w]), z_g[centre, qkv_w + vw:qkv_w + vw + LANES],
                   cw_ref, gaup_ref, gab_ref, gpack_ref, la_ref, kw=kw)


def _front(x, nw, w_r, w_g, rwkv_consts, gla_consts, width, kw, vw):
    bsz, t_len, d = x.shape
    tp = ROW_TILE
    consts = (nw, w_r, w_g) + tuple(rwkv_consts) + tuple(gla_consts)
    big = lambda arr: arr.size * arr.dtype.itemsize >= (1 << 20)
    out = lambda w: pl.BlockSpec((1, tp, w), lambda b, t: (b, t, 0))
    shape = lambda w, dt: jax.ShapeDtypeStruct((bsz, t_len, w), dt)
    return pl.pallas_call(
        functools.partial(_front_kernel, width=width, kw=kw, vw=vw),
        grid=(bsz, t_len // tp),
        in_specs=_halo_specs(tp, d, t_len) + [_const_spec(c, 2, single_buffer=big(c)) for c in consts],
        out_specs=[out(6 * width), out(2 * width), out(2 * kw + vw), out(2 * kw), out(vw)],
        out_shape=[shape(6 * width, BF16), shape(2 * width, F32), shape(2 * kw + vw, BF16),
                   shape(2 * kw, F32), shape(vw, BF16)],
        compiler_params=pltpu.CompilerParams(dimension_semantics=("parallel", "parallel"),
                                             vmem_limit_bytes=VMEM_LIMIT),
        name="front",
    )(x, x, x, *consts)


def _rwkv_prep_body(z, z_dn, z_up, mu_ref, w0_ref, wup_ref, a0_ref, aup_ref, gup_ref,
                    kk_ref, ka_ref, seg_ref, trif_ref, trib_ref, pack_ref, g_ref, *, width):
    w = width
    zs = z + mu_ref[...] * (0.5 * (z_dn + z_up) - z)
    r = zs[:, 0:w]
    k = zs[:, w:2 * w]
    v = zs[:, 2 * w:3 * w]
    w_lo = zs[:, 3 * w:3 * w + LANES]
    a_lo = zs[:, 3 * w + LANES:3 * w + 2 * LANES]
    g_lo = zs[:, 3 * w + 2 * LANES:3 * w + 3 * LANES]
    lw = -DECAY_SCALE * _sigmoid(w0_ref[...] + _mm(jnp.tanh(w_lo), wup_ref[...]))
    g_ref[0, :, 0:w] = _mm_split_lhs(trif_ref[...], lw[:, 0:w])
    g_ref[0, :, w:2 * w] = _mm_split_lhs(trib_ref[...], lw[:, w:2 * w])
    a = _sigmoid(a0_ref[...] + _mm(a_lo, aup_ref[...]))
    g = _mm(_sigmoid(g_lo), gup_ref[...])
    kk = k * kk_ref[...]
    ss = _mm(kk * kk, seg_ref[...])
    kk = kk * lax.rsqrt(jnp.maximum(ss, 1e-24))
    k = k * (1.0 + (a - 1.0) * ka_ref[...])
    for j, val in enumerate((r, k, v, kk, kk * a, g)):
        pack_ref[0, :, j * w:(j + 1) * w] = val.astype(pack_ref.dtype)


def _chunk_tri_tiles():
    chunk_id = jnp.arange(ROW_TILE) // CHUNK
    pos = jnp.arange(ROW_TILE)
    same = chunk_id[:, None] == chunk_id[None, :]
    tri_f = (same & (pos[None, :] <= pos[:, None])).astype(BF16)
    tri_b = (same & (pos[None, :] >= pos[:, None])).astype(BF16)
    return tri_f, tri_b


def _delta_chunk_problems(refs, row, reverse, out_ref, slot0):
    pack_ref, g_ref = refs
    w = g_ref.shape[-1]
    r_ref, k_ref, v_ref, kk_ref, b_ref = (pack_ref.at[:, :, j * w:(j + 1) * w] for j in range(5))
    c = CHUNK
    g = g_ref[row]
    trow = lax.broadcasted_iota(jnp.int32, (c, 1), 0)
    if reverse:
        total = g[0:1, :]
        g_excl = jnp.where(trow == c - 1, 0.0, pltpu.roll(g, c - 1, 0))
    else:
        total = g[c - 1:c, :]
        g_excl = jnp.where(trow == 0, 0.0, pltpu.roll(g, 1, 0))
    einv = jnp.exp(-g)
    ec = jnp.exp(total - g)
    kk = kk_ref[row].astype(F32)
    bb = b_ref[row].astype(F32)
    kx = k_ref[row].astype(F32)
    full = dict(a=(-kk * jnp.exp(g_excl)).astype(BF16), r=(r_ref[row].astype(F32) * jnp.exp(g)).astype(BF16),
                bt=(bb * einv).astype(BF16), kt=(kx * einv).astype(BF16), v=v_ref[row],
                bh=(bb * ec).astype(BF16), kh=(kx * ec).astype(BF16))
    head0 = lax.broadcasted_iota(jnp.int32, (c, LANES), 1) < RWKV_HEAD_DIM
    tt = lax.broadcasted_iota(jnp.int32, (c, LANES), 0)
    ss = lax.broadcasted_iota(jnp.int32, (c, LANES), 1) % c
    incl = (ss >= tt) if reverse else (ss <= tt)
    strict = (ss > tt) if reverse else (ss < tt)
    probs = []
    for p in range(total.shape[1] // LANES):
        sl = slice(LANES * p, LANES * (p + 1))
        q = {name: z[:, sl] for name, z in full.items()}
        gam_col = jnp.transpose(jnp.broadcast_to(jnp.exp(total[:, sl]), (LANES, LANES)))
        q.update(head0=head0, incl=incl, strict=strict, gam_col=gam_col, lanes=sl, row=row, out=out_ref,
                 slot=slot0 + p)
        probs.append(q)
    return probs


def _rwkv_scan_kernel(*refs, rows):
    in_f, in_b = refs[0:2], refs[2:4]
    yf_ref, yb_ref, h_ref = refs[4:7]

    @pl.when(pl.program_id(1) == 0)
    def _():
        h_ref[...] = jnp.zeros_like(h_ref)

    c = CHUNK
    n = LANES
    n_pairs = yf_ref.shape[-1] // LANES
    probs = []
    for row in range(rows):
        probs += _delta_chunk_problems(in_f, row, False, yf_ref, (2 * row) * n_pairs)
        probs += _delta_chunk_problems(in_b, row, True, yb_ref, (2 * row + 1) * n_pairs)
    stack = lambda z, q: _pair_stack(z.astype(BF16), q["head0"])
    rr = lax.broadcasted_iota(jnp.int32, (n, n), 0) // c
    cc = lax.broadcasted_iota(jnp.int32, (n, n), 1) // c
    same_head = rr == cc
    zero = jnp.zeros((c, n), F32)
    ar = [jnp.concatenate([q["a"], q["r"]], axis=0) for q in probs]
    gm = [_mm_nt(x, jnp.concatenate([stack(q["bt"], q), stack(q["kt"], q)], axis=0)) for x, q in zip(ar, probs)]
    hs = [h_ref[q["slot"]] for q in probs]
    hb = [h.astype(BF16) for h in hs]
    rh = [_mm(q["r"], h) for q, h in zip(probs, hb)]
    a_ab = [jnp.where(q["strict"], m[:c, :n], zero) for q, m in zip(probs, gm)]
    a_ak = [jnp.where(q["strict"], m[:c, n:], zero) for q, m in zip(probs, gm)]
    a_rbk = [jnp.where(jnp.concatenate([q["incl"], q["incl"]], axis=1), m[c:], 0.0) for q, m in zip(probs, gm)]
    vs = [stack(q["v"], q) for q in probs]
    zs = [_mm(jnp.concatenate([q["a"], a.astype(BF16)], axis=1), jnp.concatenate([h, v], axis=0))
          for q, a, h, v in zip(probs, a_ak, hb, vs)]
    ps = a_ab
    levels = CHUNK.bit_length() - 1
    for level in range(levels - 1):
        prods = [_mm(p, jnp.concatenate([stack(z, q), stack(p, q)], axis=1)) for p, z, q in zip(ps, zs, probs)]
        zs = [z + pr[:, :n] for z, pr in zip(zs, prods)]
        ps = [pr[:, n:] for pr in prods]
    us = [z + _mm(p, stack(z, q)) for p, z, q in zip(ps, zs, probs)]
    ys = [x + _mm(a, jnp.concatenate([stack(u, q), v], axis=0))
          for x, a, u, v, q in zip(rh, a_rbk, us, vs, probs)]
    upd = [_mm_tn(jnp.concatenate([q["bh"], q["kh"]], axis=0), jnp.concatenate([u.astype(BF16), q["v"]], axis=0))
           for q, u in zip(probs, us)]
    for q, y, h, d in zip(probs, ys, hs, upd):
        q["out"][q["row"], :, q["lanes"]] = y
        h_ref[q["slot"]] = h * q["gam_col"] + jnp.where(same_head, d, 0.0)


def _rwkv_scan(pack, g, width, rows):
    bsz, t_len, _ = pack.shape
    nc = t_len // CHUNK
    fwd = lambda j: pl.BlockSpec((rows, CHUNK, width), lambda b, t: (b, t, j))
    bwd = lambda j: pl.BlockSpec((rows, CHUNK, width), lambda b, t: (b, nc - 1 - t, j))
    n_pairs = width // LANES
    y_shape = jax.ShapeDtypeStruct((bsz, t_len, width), F32)
    return pl.pallas_call(
        functools.partial(_rwkv_scan_kernel, rows=rows),
        grid=(bsz // rows, nc),
        in_specs=[pl.BlockSpec((rows, CHUNK, 5 * width), lambda b, t: (b, t, 0)), fwd(0),
                  pl.BlockSpec((rows, CHUNK, 5 * width), lambda b, t: (b, nc - 1 - t, 0)), bwd(1)],
        out_specs=[fwd(0), bwd(0)],
        out_shape=[y_shape, y_shape],
        scratch_shapes=[pltpu.VMEM((2 * rows * n_pairs, LANES, LANES), F32)],
        compiler_params=pltpu.CompilerParams(dimension_semantics=("parallel", "arbitrary"),
                                             vmem_limit_bytes=VMEM_LIMIT),
        name="rwkv_scan",
    )(pack, g, pack, g)


def _gla_prep_body(x, x_dn, x_up, a_lo, cw_ref, aup_ref, ab_ref, pack_ref, la_ref, *, kw):
    qkv_w = x.shape[1]
    qkv = _silu(cw_ref[0:1, :] * x_dn + cw_ref[1:2, :] * x + cw_ref[2:3, :] * x_up)
    logit = _mm(a_lo, aup_ref[...]) + ab_ref[...]
    la_ref[0] = -_softplus(-logit) * (1.0 / GLA_GATE_TEMP)
    dk = kw // GLA_HEADS
    pack_ref[0, :, 0:kw] = (qkv[:, 0:kw] * (dk ** -0.5)).astype(pack_ref.dtype)
    pack_ref[0, :, kw:qkv_w] = qkv[:, kw:qkv_w].astype(pack_ref.dtype)


def _gla_block_problems(refs, seq, reverse, slot0, out_ref):
    qkv_ref, la_ref = refs
    kw = la_ref.shape[-1]
    q_ref, k_ref, v_ref = qkv_ref.at[:, :, 0:kw], qkv_ref.at[:, :, kw:2 * kw], qkv_ref.at[:, :, 2 * kw:]
    c = CHUNK
    nb = la_ref.shape[1] // c
    tri_bf = _chunk_causal(reverse).astype(BF16)
    tt = lax.broadcasted_iota(jnp.int32, (c, LANES), 0)
    ss = lax.broadcasted_iota(jnp.int32, (c, LANES), 1) % c
    causal = (ss >= tt) if reverse else (ss <= tt)
    probs = []
    order = range(nb - 1, -1, -1) if reverse else range(nb)
    for step, ci in enumerate(order):
        rows = slice(ci * c, (ci + 1) * c)
        la = la_ref[seq, rows, :]
        bc = _mm_split_lhs(tri_bf, la)
        total = bc[0:1, :] if reverse else bc[c - 1:c, :]
        kx = k_ref[seq, rows, :].astype(F32)
        q_t = (q_ref[seq, rows, :].astype(F32) * jnp.exp(bc)).astype(BF16)
        k_t = (kx * jnp.exp(-bc)).astype(BF16)
        k_h = (kx * jnp.exp(total - bc)).astype(BF16)
        lowest = jnp.min(total)
        for p in range(la.shape[1] // LANES):
            sl = slice(LANES * p, LANES * (p + 1))
            vsl = slice(2 * LANES * p, 2 * LANES * (p + 1))
            gam = jnp.transpose(jnp.broadcast_to(jnp.exp(total[:, sl]), (LANES, LANES)))
            probs.append(dict(q=q_t[:, sl], kt=k_t[:, sl], kh=k_h[:, sl], v=v_ref[seq, rows, vsl], causal=causal,
                              gam_col=jnp.concatenate([gam, gam], axis=1), slot=slot0 + p, step=step, rows=rows,
                              lanes=vsl, out=out_ref, reverse=reverse, lowest=lowest, bc=bc[:, sl],
                              k_raw=kx[:, sl], q_ref=q_ref, key_lanes=sl, seq=seq))
    return probs


def _gla_intra_pairwise(q, inter, rob_ref):
    c = CHUNK
    bc = q["bc"]
    q_raw = q["q_ref"][q["seq"], q["rows"], q["key_lanes"]].astype(F32)
    rob_ref[0] = bc
    rob_ref[1] = q["k_raw"]
    rob_ref[2] = q["v"][:, :LANES].astype(F32)
    rob_ref[3] = q["v"][:, LANES:].astype(F32)
    trow = lax.broadcasted_iota(jnp.int32, (c, 1), 0)
    head0 = lax.broadcasted_iota(jnp.int32, (c, LANES), 1) < (LANES // 2)

    def body(j, acc):
        acc0, acc1 = acc
        b_j = rob_ref[0, pl.ds(j, 1), :]
        k_j = rob_ref[1, pl.ds(j, 1), :]
        visible = (trow <= j) if q["reverse"] else (trow >= j)
        pj = q_raw * k_j * jnp.exp(jnp.where(visible, bc - b_j, -jnp.inf))
        s0 = jnp.sum(jnp.where(head0, pj, 0.0), axis=1, keepdims=True)
        s1 = jnp.sum(jnp.where(head0, 0.0, pj), axis=1, keepdims=True)
        return acc0 + s0 * rob_ref[2, pl.ds(j, 1), :], acc1 + s1 * rob_ref[3, pl.ds(j, 1), :]

    zero = jnp.zeros((c, LANES), F32)
    acc0, acc1 = lax.fori_loop(0, c, body, (zero, zero))
    q["out"][q["seq"], q["rows"], q["lanes"]] = inter + jnp.concatenate([acc0, acc1], axis=1)


def _gla_scan_kernel(*refs):
    in_f, in_b = refs[0:2], refs[2:4]
    of_ref, ob_ref, st_ref, rob_ref = refs[4:8]

    @pl.when(pl.program_id(1) == 0)
    def _():
        st_ref[...] = jnp.zeros_like(st_ref)

    c = CHUNK
    n_slots = st_ref.shape[0]
    n_seq = of_ref.shape[0]
    n_pairs = n_slots // (2 * n_seq)
    probs = []
    for seq in range(n_seq):
        probs += _gla_block_problems(in_f, seq, False, (2 * seq) * n_pairs, of_ref)
        probs += _gla_block_problems(in_b, seq, True, (2 * seq + 1) * n_pairs, ob_ref)
    n_steps = 1 + max(q["step"] for q in probs)
    head0_k = lax.broadcasted_iota(jnp.int32, (c, LANES), 1) < (LANES // 2)
    head0_v = lax.broadcasted_iota(jnp.int32, (c, 2 * LANES), 1) < LANES
    same_head = (lax.broadcasted_iota(jnp.int32, (LANES, 2 * LANES), 0) // (LANES // 2)
                 == lax.broadcasted_iota(jnp.int32, (LANES, 2 * LANES), 1) // LANES)
    scores = [jnp.where(q["causal"], _mm_nt(q["q"], _pair_stack(q["kt"], head0_k)), 0.0) for q in probs]
    upd = [jnp.where(same_head, _mm_tn(q["kh"], q["v"]), 0.0) for q in probs]
    state = [st_ref[s] for s in range(n_slots)]
    entering = [None] * len(probs)
    for step in range(n_steps):
        for i, q in enumerate(probs):
            if q["step"] == step:
                entering[i] = state[q["slot"]]
                state[q["slot"]] = state[q["slot"]] * q["gam_col"] + upd[i]
    outs = [_mm(jnp.concatenate([s.astype(BF16), q["q"]], axis=1),
                jnp.concatenate([_pair_stack(q["v"], head0_v), st.astype(BF16)], axis=0))
            for s, q, st in zip(scores, probs, entering)]
    for q, o in zip(probs, outs):
        q["out"][q["seq"], q["rows"], q["lanes"]] = o
    for s in range(n_slots):
        st_ref[s] = state[s]

    lowest = functools.reduce(jnp.minimum, [q["lowest"] for q in probs])

    @pl.when(lowest < -GLA_FACTORISED_LOG_RANGE)
    def _():
        for q, st in zip(probs, entering):
            _gla_intra_pairwise(q, _mm(q["q"], st), rob_ref)


def _gla_scan(pack, la, kw, vw, n_seq):
    bsz, t_len, _ = pack.shape
    rows = GLA_BLOCK_CHUNKS * CHUNK
    nb = t_len // rows
    fwd = lambda w, j: pl.BlockSpec((n_seq, rows, w), lambda b, t: (b, t, j))
    bwd = lambda w, j: pl.BlockSpec((n_seq, rows, w), lambda b, t: (b, nb - 1 - t, j))
    o_shape = jax.ShapeDtypeStruct((bsz, t_len, vw), F32)
    specs = lambda blk, la_col: [blk(2 * kw + vw, 0), blk(kw, la_col)]
    return pl.pallas_call(
        _gla_scan_kernel,
        grid=(bsz // n_seq, nb),
        in_specs=specs(fwd, 0) + specs(bwd, 1),
        out_specs=[fwd(vw, 0), bwd(vw, 0)],
        out_shape=[o_shape, o_shape],
        scratch_shapes=[pltpu.VMEM((2 * n_seq * (kw // LANES), LANES, 2 * LANES), F32),
                        pltpu.VMEM((4, CHUNK, LANES), F32)],
        compiler_params=pltpu.CompilerParams(dimension_semantics=("parallel", "arbitrary"),
                                             vmem_limit_bytes=VMEM_LIMIT),
        name="gla_scan",
    )(pack, la, pack, la)


def _tail_kernel(x_ref, yf_ref, yb_ref, rkv_ref, g_ref, of_ref, ob_ref, zg_ref, p_ref,
                 seg_ref, rk_ref, lnw_ref, lnb_ref, gn_ref, wor_ref, wog_ref, nmix_ref,
                 nfpre_ref, nfpost_ref, wgate_ref, wup_ref, wdown_ref, wpp_ref, wpg_ref, bpg_ref, nple_ref,
                 out_ref):
    dot = functools.partial(jnp.dot, preferred_element_type=F32)
    tm = x_ref.shape[0]
    gw = g_ref.shape[1]
    r_ref, k_ref, v_ref = (rkv_ref.at[:, j * gw:(j + 1) * gw] for j in range(3))
    halves = [slice(0, tm // 2), slice(tm // 2, tm)]
    both = lambda fn, *lists: [fn(*args) for args in zip(*lists)]
    seg = seg_ref[...]
    inv_n = 1.0 / RWKV_HEAD_DIM
    y = [yf_ref[s, :] + yb_ref[s, :] for s in halves]
    mu = [_mm(a, seg) * inv_n for a in y]
    yc = both(lambda a, b: a - b, y, mu)
    var = [_mm(a * a, seg) * inv_n for a in yc]
    bonus_sum = [_mm(r_ref[s, :].astype(F32) * k_ref[s, :].astype(F32) * rk_ref[...], seg) for s in halves]
    y_r = [((c * lax.rsqrt(v + RWKV_GN_EPS) * lnw_ref[...] + lnb_ref[...] + b * v_ref[s, :].astype(F32))
            * g_ref[s, :].astype(F32)).astype(BF16) for c, v, b, s in zip(yc, var, bonus_sum, halves)]
    o = [of_ref[s, :] + ob_ref[s, :] for s in halves]
    gate_g = [zg_ref[s, :].astype(F32) for s in halves]
    y_g = [jnp.concatenate(
        [_rms(a[:, LANES * h:LANES * (h + 1)], gn_ref[...]) * _silu(b[:, LANES * h:LANES * (h + 1)])
         for h in range(GLA_HEADS)], axis=1).astype(BF16) for a, b in zip(o, gate_g)]
    mix = both(lambda a, b: dot(a, wor_ref[...]) + dot(b, wog_ref[...]), y_r, y_g)
    h1 = [x_ref[s, :] + _rms(a, nmix_ref[...]) for a, s in zip(mix, halves)]
    hn = [_rms(a, nfpre_ref[...]).astype(BF16) for a in h1]
    ffn_gate = [dot(a, wgate_ref[...]) for a in hn]
    ffn_up = [dot(a, wup_ref[...]) for a in hn]
    act = both(lambda a, b: (_silu(a) * b).astype(BF16), ffn_gate, ffn_up)
    down = [dot(a, wdown_ref[...]) for a in act]
    h2 = both(lambda a, b: a + _rms(b, nfpost_ref[...]), h1, down)
    e = [dot(p_ref[s, :].astype(BF16), wpp_ref[...]) for s in halves]
    gate = [_sigmoid(dot(a.astype(BF16), wpg_ref[...]) + bpg_ref[...]) for a in h2]
    for a, b, c, s in zip(h2, gate, e, halves):
        out_ref[s, :] = a + _rms(b * c, nple_ref[...])


def _tail(x2, y_f, y_b, pack, o_f, o_b, ggate, p2, consts):
    m, d = x2.shape
    tm = ROW_TILE
    w = y_f.shape[1]
    row = lambda width, j=0: pl.BlockSpec((tm, width), lambda i: (i, j))
    big = lambda arr: arr.size * arr.dtype.itemsize >= (1 << 20)
    return pl.pallas_call(
        _tail_kernel,
        grid=(m // tm,),
        in_specs=[row(d), row(w), row(w), row(3 * w, 0), row(w, 5), row(w), row(w),
                  row(w), row(p2.shape[1])] + [_const_spec(c, 1, single_buffer=big(c)) for c in consts],
        out_specs=row(d),
        out_shape=jax.ShapeDtypeStruct((m, d), F32),
        compiler_params=pltpu.CompilerParams(dimension_semantics=("parallel",),
                                             vmem_limit_bytes=VMEM_LIMIT),
        name="tail",
    )(x2, y_f, y_b, pack, pack, o_f, o_b, ggate, p2, *consts)


def _pad_cols(w, n):
    return jnp.pad(w, ((0, 0), (0, n - w.shape[1])))


def _layer(h, p_i, norm_mix_pre, norm_mix_post, norm_ffn_pre, norm_ffn_post, norm_ple,
           w_in, rwkv_mu, rwkv_w0, rwkv_w_up, rwkv_a0, rwkv_a_up, rwkv_g_up,
           rwkv_k_k, rwkv_k_a, rwkv_r_k, rwkv_ln_w, rwkv_ln_b,
           gla_conv, gla_a_up, gla_a_b, gla_norm, w_out,
           ffn_gate, ffn_up, ffn_down, ple_proj, ple_gate, ple_gate_b):
    bsz, t_len, d = h.shape
    m = bsz * t_len
    rw_w = rwkv_w0.shape[1]
    kw = gla_a_b.shape[1]
    vw = w_out.shape[0] - rw_w
    assert vw == rw_w and t_len % (GLA_BLOCK_CHUNKS * CHUNK) == 0 and m % ROW_TILE == 0
    rwkv_in = 3 * rw_w + 2 * RWKV_DECAY_LORA + RWKV_AAA_LORA + RWKV_GATE_LORA
    row = lambda vec: vec.reshape(1, -1)

    lo_w = 3 * rw_w + 2 * RWKV_DECAY_LORA
    a_end = lo_w + RWKV_AAA_LORA

    def rwkv_cols(wmat):
        return jnp.concatenate([wmat[:, :lo_w], _pad_cols(wmat[:, lo_w:a_end], LANES),
                                wmat[:, a_end:rwkv_in]], axis=1)

    w_r = rwkv_cols(w_in[:, :rwkv_in]).astype(BF16)
    mu = rwkv_cols(row(rwkv_mu))
    gla_in = w_in.shape[1] - rwkv_in
    gla_main = gla_in - 2 * GLA_GATE_LORA
    w_g = jnp.concatenate([w_in[:, rwkv_in:rwkv_in + gla_main],
                           _pad_cols(w_in[:, rwkv_in + gla_main:], LANES)], axis=1).astype(BF16)

    zeros = jnp.zeros((RWKV_DECAY_LORA, rw_w), F32)
    w_up = jnp.concatenate([jnp.concatenate([rwkv_w_up[0], zeros], axis=1),
                            jnp.concatenate([zeros, rwkv_w_up[1]], axis=1)], axis=0).astype(BF16)
    a_up = jnp.pad(rwkv_a_up, ((0, LANES - RWKV_AAA_LORA), (0, 0))).astype(BF16)
    g_up = rwkv_g_up.astype(BF16)
    gz = jnp.zeros((GLA_GATE_LORA, kw), F32)
    gla_up = jnp.concatenate([jnp.concatenate([gla_a_up[0], gz], axis=1),
                              jnp.concatenate([gz, gla_a_up[1]], axis=1)], axis=0)
    gla_up = jnp.pad(gla_up, ((0, LANES - 2 * GLA_GATE_LORA), (0, 0))).astype(BF16)
    head_id = jnp.arange(rw_w) // RWKV_HEAD_DIM
    seg = (head_id[:, None] == head_id[None, :]).astype(BF16)

    x2 = h.reshape(m, d)
    rwkv_consts = (mu, rwkv_w0.reshape(1, -1), w_up, row(rwkv_a0), a_up, g_up, row(rwkv_k_k), row(rwkv_k_a),
                   seg) + _chunk_tri_tiles()
    gla_consts = (gla_conv.reshape(gla_conv.shape[0], -1), gla_up, gla_a_b.reshape(1, -1))
    pack, lw, gpack, la, ggate = _front(h, row(norm_mix_pre), w_r, w_g, rwkv_consts, gla_consts, rw_w, kw, vw)
    y_f, y_b = _rwkv_scan(pack, lw, rw_w, next(r for r in (RWKV_SCAN_ROWS, 2, 1) if bsz % r == 0))
    o_f, o_b = _gla_scan(gpack, la, kw, vw, next(r for r in (GLA_SCAN_ROWS, 1) if bsz % r == 0))

    w_o = w_out.astype(BF16)
    consts = (seg, row(rwkv_r_k), row(rwkv_ln_w), row(rwkv_ln_b), row(gla_norm), w_o[:rw_w], w_o[rw_w:],
              row(norm_mix_post), row(norm_ffn_pre), row(norm_ffn_post), ffn_gate.astype(BF16),
              ffn_up.astype(BF16), ffn_down.astype(BF16), ple_proj.astype(BF16), ple_gate.astype(BF16),
              row(ple_gate_b), row(norm_ple))
    out = _tail(x2, y_f.reshape(m, rw_w), y_b.reshape(m, rw_w), pack.reshape(m, -1), o_f.reshape(m, vw),
                o_b.reshape(m, vw), ggate.reshape(m, vw), p_i.reshape(m, -1), consts)
    return out.reshape(bsz, t_len, d)


def kernel(x, p, norm_mix_pre, norm_mix_post, norm_ffn_pre, norm_ffn_post, norm_ple, w_in, rwkv_mu, rwkv_w0, rwkv_w_up, rwkv_a0, rwkv_a_up, rwkv_g_up, rwkv_k_k, rwkv_k_a, rwkv_r_k, rwkv_ln_w, rwkv_ln_b, gla_conv, gla_a_up, gla_a_b, gla_norm, w_out, ffn_gate, ffn_up, ffn_down, ple_proj, ple_gate, ple_gate_b):
    params = (norm_mix_pre, norm_mix_post, norm_ffn_pre, norm_ffn_post, norm_ple, w_in, rwkv_mu,
              rwkv_w0, rwkv_w_up, rwkv_a0, rwkv_a_up, rwkv_g_up, rwkv_k_k, rwkv_k_a, rwkv_r_k,
              rwkv_ln_w, rwkv_ln_b, gla_conv, gla_a_up, gla_a_b, gla_norm, w_out, ffn_gate, ffn_up,
              ffn_down, ple_proj, ple_gate, ple_gate_b)
    h = x
    for i in range(p.shape[0]):
        h = _layer(h, p[i], *(w[i] for w in params))
    return h
```
